```python
import jax, jax.numpy as jnp
from jax import lax
import numpy as np

D_MODEL = 1024
BATCH = 8
SEQ = 2048
DEPTH = 2
DEC_BATCH = 128
DEC_SEQ = 1
PAST_LEN = 16384
PAGE_SIZE = 128

D_MIX = D_MODEL
D_POOL = D_MIX // 4
POOL_WINDOWS = (2, 4, 8, 16)
N_POOL_GROUPS = len(POOL_WINDOWS)
POOL_GROUP = D_POOL // N_POOL_GROUPS
POOL_BUF = max(POOL_WINDOWS) - 1
D_CONV = D_MIX // 4
CONV_WIDTH = 31
CONV_BUF = CONV_WIDTH - 1
D_GMLP = D_MIX - D_POOL - D_CONV
GMLP_HEAD = 64
N_GMLP_HEADS = D_GMLP // GMLP_HEAD
CHUNK = 128
D_IN = D_POOL + 2 * D_CONV + 2 * D_GMLP
D_FF = -(-(8 * D_MODEL) // (3 * 256)) * 256
N_MOD = 6
EPS = 1e-6

kernel_name = "hybrid_pool_conv_gmlp_decoder_step"


def rms_norm(x, g):
    xf = x.astype(jnp.float32)
    y = xf * lax.rsqrt(jnp.mean(xf * xf, axis=-1, keepdims=True) + EPS)
    return (y * g.astype(jnp.float32)).astype(x.dtype)


def layer_norm(x, g, b):
    xf = x.astype(jnp.float32)
    mu = jnp.mean(xf, axis=-1, keepdims=True)
    var = jnp.mean(jnp.square(xf - mu), axis=-1, keepdims=True)
    y = (xf - mu) * lax.rsqrt(var + EPS) * g.astype(jnp.float32) + b.astype(jnp.float32)
    return y.astype(x.dtype)


def pool_mixer(xa, buf, start, w_grp, scale):
    n, L, _ = xa.shape
    ext = jnp.concatenate([buf.astype(xa.dtype), xa], axis=1)
    cs = jnp.cumsum(ext.astype(jnp.float32), axis=1)
    cs = jnp.pad(cs, ((0, 0), (1, 0), (0, 0)))
    pos = start + jnp.arange(L)
    means = []
    for gi, w in enumerate(POOL_WINDOWS):
        sl = slice(gi * POOL_GROUP, (gi + 1) * POOL_GROUP)
        wsum = cs[:, POOL_BUF + 1:POOL_BUF + 1 + L, sl] - cs[:, POOL_BUF + 1 - w:POOL_BUF + 1 - w + L, sl]
        cnt = jnp.minimum(pos + 1, w).astype(jnp.float32)[None, :, None]
        means.append(wsum / cnt)
    mean = jnp.stack(means, axis=2)
    d = mean - xa.astype(jnp.float32).reshape(n, L, N_POOL_GROUPS, POOL_GROUP)
    y = jnp.einsum('nlgp,gpq->nlgq', d, w_grp.astype(jnp.float32)).reshape(n, L, D_POOL)
    y = y * scale.astype(jnp.float32)
    return y.astype(xa.dtype), ext[:, -POOL_BUF:]


def conv_module(a, gt, buf, w_dw, b_dw, ln_g, ln_b):
    glu = a * jax.nn.sigmoid(gt)
    ext = jnp.concatenate([buf.astype(glu.dtype), glu], axis=1)
    y = lax.conv_general_dilated(ext, w_dw[:, None, :].astype(ext.dtype), window_strides=(1,),
                                 padding='VALID', dimension_numbers=('NWC', 'WIO', 'NWC'),
                                 feature_group_count=D_CONV) + b_dw
    y = jax.nn.silu(layer_norm(y, ln_g, ln_b))
    return y, ext[:, -CONV_BUF:]


def chunk_gmlp(u, v, w_s, b_s, ln_g, ln_b):
    v = layer_norm(v, ln_g, ln_b)
    n, L, _ = v.shape
    Lp = -(-L // CHUNK) * CHUNK
    vp = jnp.pad(v, ((0, 0), (0, Lp - L), (0, 0))).reshape(n, Lp // CHUNK, CHUNK, N_GMLP_HEADS, GMLP_HEAD)
    mask = jnp.tril(jnp.ones((CHUNK, CHUNK), dtype=bool))
    ws = jnp.where(mask[None], w_s, jnp.zeros_like(w_s)).astype(vp.dtype)
    z = jnp.einsum('hij,ncjhd->ncihd', ws, vp) + b_s.T[None, None, :, :, None]
    z = z.reshape(n, Lp, D_GMLP)[:, :L]
    return u * z, v


def trunk_layer(x, c, pool_buf, conv_buf, start, w_ada, b_ada, norm1_g, norm2_g, w_in, pool_w,
                pool_scale, conv_w, conv_b, conv_ln_g, conv_ln_b, gmlp_ln_g, gmlp_ln_b, gmlp_ws,
                gmlp_bs, w_out, w_ff1, w_ff3, w_ff2):
    mod = jax.nn.silu(c) @ w_ada + b_ada
    sh1, sc1, g1, sh2, sc2, g2 = jnp.split(mod[:, None, :], N_MOD, axis=-1)
    h = rms_norm(x, norm1_g) * (1 + sc1) + sh1
    p = h @ w_in
    xa, a, gt, u, v = jnp.split(p, [D_POOL, D_POOL + D_CONV, D_POOL + 2 * D_CONV,
                                    D_POOL + 2 * D_CONV + D_GMLP], axis=-1)
    ya, new_pool = pool_mixer(xa, pool_buf, start, pool_w, pool_scale)
    yb, new_conv = conv_module(a, gt, conv_buf, conv_w, conv_b, conv_ln_g, conv_ln_b)
    yc, v_rows = chunk_gmlp(jax.nn.gelu(u), jax.nn.gelu(v), gmlp_ws, gmlp_bs, gmlp_ln_g, gmlp_ln_b)
    y = jnp.concatenate([ya, yb, yc], axis=-1) @ w_out
    x = x + g1 * y
    h = rms_norm(x, norm2_g) * (1 + sc2) + sh2
    ff = (jax.nn.silu(h @ w_ff1) * (h @ w_ff3)) @ w_ff2
    x = x + g2 * ff
    return x, new_pool, new_conv, v_rows


def setup_inputs(seed: int = 0) -> dict:
    key = jax.random.key(seed)
    ks = jax.random.split(key, 32)
    f32 = jnp.float32
    nrm = lambda k, shape, s: (jax.random.normal(k, shape, f32) * s)
    D = D_MODEL
    return {
        "x_prompt": nrm(ks[0], (BATCH, SEQ, D), 1.0),
        "x_sample": nrm(ks[1], (DEC_BATCH, DEC_SEQ, D), 1.0),
        "c_prompt": nrm(ks[2], (BATCH, D), 1.0),
        "c_sample": nrm(ks[3], (DEC_BATCH, D), 1.0),
        "state_pool": nrm(ks[4], (DEPTH, DEC_BATCH, POOL_BUF, D_POOL), 1.0),
        "state_conv": nrm(ks[5], (DEPTH, DEC_BATCH, CONV_BUF, D_CONV), 0.5),
        "w_ada": nrm(ks[6], (DEPTH, D, N_MOD * D), D ** -0.5),
        "b_ada": nrm(ks[7], (DEPTH, N_MOD * D), 0.02),
        "norm1_g": 1.0 + nrm(ks[8], (DEPTH, D), 0.05),
        "norm2_g": 1.0 + nrm(ks[9], (DEPTH, D), 0.05),
        "w_in": nrm(ks[10], (DEPTH, D, D_IN), D ** -0.5),
        "pool_w": nrm(ks[11], (DEPTH, N_POOL_GROUPS, POOL_GROUP, POOL_GROUP), POOL_GROUP ** -0.5),
        "pool_scale": 1.0 + nrm(ks[12], (DEPTH, D_POOL), 0.1),
        "conv_w": nrm(ks[13], (DEPTH, CONV_WIDTH, D_CONV), CONV_WIDTH ** -0.5),
        "conv_b": nrm(ks[14], (DEPTH, D_CONV), 0.02),
        "conv_ln_g": 1.0 + nrm(ks[15], (DEPTH, D_CONV), 0.05),
        "conv_ln_b": nrm(ks[16], (DEPTH, D_CONV), 0.02),
        "gmlp_ln_g": 1.0 + nrm(ks[17], (DEPTH, D_GMLP), 0.05),
        "gmlp_ln_b": nrm(ks[18], (DEPTH, D_GMLP), 0.02),
        "gmlp_ws": nrm(ks[19], (DEPTH, N_GMLP_HEADS, CHUNK, CHUNK), CHUNK ** -0.5),
        "gmlp_bs": 1.0 + nrm(ks[20], (DEPTH, N_GMLP_HEADS, CHUNK), 0.1),
        "w_out": nrm(ks[21], (DEPTH, D_MIX, D), D_MIX ** -0.5),
        "w_ff1": nrm(ks[22], (DEPTH, D, D_FF), D ** -0.5),
        "w_ff3": nrm(ks[23], (DEPTH, D, D_FF), D ** -0.5),
        "w_ff2": nrm(ks[24], (DEPTH, D_FF, D), D_FF ** -0.5),
        "final_g": 1.0 + nrm(ks[25], (D,), 0.05),
    }


def reference(x_prompt, x_sample, c_prompt, c_sample, state_pool, state_conv, w_ada, b_ada,
              norm1_g, norm2_g, w_in, pool_w, pool_scale, conv_w, conv_b, conv_ln_g, conv_ln_b,
              gmlp_ln_g, gmlp_ln_b, gmlp_ws, gmlp_bs, w_out, w_ff1, w_ff3, w_ff2, final_g):
    xp, xs = x_prompt, x_sample
    nb = xp.shape[0]
    zero_pool = jnp.zeros((nb, POOL_BUF, D_POOL), xp.dtype)
    zero_conv = jnp.zeros((nb, CONV_BUF, D_CONV), xp.dtype)
    pool_p, conv_p, pool_s, conv_s, v_s = [], [], [], [], []
    for l in range(DEPTH):
        w = (w_ada[l], b_ada[l], norm1_g[l], norm2_g[l], w_in[l], pool_w[l], pool_scale[l],
             conv_w[l], conv_b[l], conv_ln_g[l], conv_ln_b[l], gmlp_ln_g[l], gmlp_ln_b[l],
             gmlp_ws[l], gmlp_bs[l], w_out[l], w_ff1[l], w_ff3[l], w_ff2[l])
        xp, npool_p, nconv_p, _ = trunk_layer(xp, c_prompt, zero_pool, zero_conv, 0, *w)
        xs, npool_s, nconv_s, nv_s = trunk_layer(xs, c_sample, state_pool[l], state_conv[l], PAST_LEN, *w)
        pool_p.append(npool_p)
        conv_p.append(nconv_p)
        pool_s.append(npool_s)
        conv_s.append(nconv_s)
        v_s.append(nv_s)
    y_prompt = rms_norm(xp, final_g)
    y_sample = rms_norm(xs, final_g)
    new_pool_prompt = jnp.stack(pool_p)
    new_conv_prompt = jnp.stack(conv_p)
    new_pool_sample = jnp.stack(pool_s)
    new_conv_sample = jnp.stack(conv_s)
    new_chunkv_sample = jnp.stack(v_s)
    return (y_prompt, y_sample, new_pool_prompt, new_conv_prompt, new_pool_sample, new_conv_sample, new_chunkv_sample)
```

```python
import functools

import jax
import jax.numpy as jnp
from jax.experimental import pallas as pl
from jax.experimental.pallas import tpu as pltpu

D_MODEL = 1024
D_POOL = 256
POOL_WINDOWS = (2, 4, 8, 16)
POOL_GROUP = 64
POOL_BUF = 15
D_CONV = 256
CONV_WIDTH = 31
CONV_BUF = 30
D_GMLP = 512
GMLP_HEAD = 64
N_GMLP_HEADS = 8
CHUNK = 128
D_IN = D_POOL + 2 * D_CONV + 2 * D_GMLP
D_FF = 2816
N_MOD = 6
EPS = 1e-6

LANES = 128
SUBLANES = 8

TOKEN_TILE = 512
FF_CHUNK = 256
ADA_COLS = 1536
POOL_HIST = 16
CONV_HIST = 32
VMEM_LIMIT = 56 * 1024 * 1024

_BF16 = jnp.bfloat16
_F32 = jnp.float32


def _dot(a, b):
    return jnp.dot(a, b, preferred_element_type=_F32)


def _sigmoid(x):
    return 1.0 / (1.0 + jnp.exp(-x))


def _silu(x):
    return x * _sigmoid(x)


def _gelu_tanh(x):
    c = 0.7978845608028654
    return 0.5 * x * (1.0 + jnp.tanh(c * (x + 0.044715 * (x * x * x))))


def _rms_norm(x, g):
    ms = jnp.mean(x * x, axis=-1, keepdims=True)
    return (x * jax.lax.rsqrt(ms + EPS)) * g


def _layer_norm(x, g, b):
    mu = jnp.mean(x, axis=-1, keepdims=True)
    xc = x - mu
    var = jnp.mean(xc * xc, axis=-1, keepdims=True)
    return xc * jax.lax.rsqrt(var + EPS) * g + b


def _ada_kernel(c_ref, w_ref, b_ref, o_ref):
    s = _silu(c_ref[...]).astype(_BF16)
    o_ref[...] = _dot(s, w_ref[...].astype(_BF16)) + b_ref[...]


def _ada_call(c_all, w_ada, b_ada):
    depth = w_ada.shape[0]
    n = c_all.shape[0]
    n_col = (N_MOD * D_MODEL) // ADA_COLS
    return pl.pallas_call(
        _ada_kernel,
        out_shape=jax.ShapeDtypeStruct((depth, n, N_MOD * D_MODEL), _F32),
        grid=(depth, n_col),
        in_specs=[
            pl.BlockSpec((n, D_MODEL), lambda l, j: (0, 0)),
            pl.BlockSpec((None, D_MODEL, ADA_COLS), lambda l, j: (l, 0, j)),
            pl.BlockSpec((None, 1, ADA_COLS), lambda l, j: (l, 0, j)),
        ],
        out_specs=pl.BlockSpec((None, n, ADA_COLS), lambda l, j: (l, 0, j)),
        compiler_params=pltpu.CompilerParams(
            dimension_semantics=("arbitrary", "arbitrary"), vmem_limit_bytes=VMEM_LIMIT),
        name="ada_mod",
    )(c_all, w_ada, b_ada.reshape(depth, 1, N_MOD * D_MODEL))


def _mixer_prompt_kernel(x_ref, sh_ref, sc_ref, g_ref, ng_ref, win_ref, poolw_ref, pools_ref,
                         convw_ref, convb_ref, clg_ref, clb_ref, glg_ref, glb_ref, wsp_ref,
                         bsf_ref, wout_ref,
                         o_ref, npool_ref, nconv_ref,
                         xa_ext, glu_ext, ycat):
    t = pl.program_id(1)
    n_t = pl.num_programs(1)
    T = TOKEN_TILE

    @pl.when(t == 0)
    def _():
        xa_ext[0:POOL_HIST, :] = jnp.zeros((POOL_HIST, D_POOL), _F32)
        glu_ext[0:CONV_HIST, :] = jnp.zeros((CONV_HIST, D_CONV), _F32)

    x = x_ref[...]
    h = _rms_norm(x, ng_ref[...]) * (1.0 + sc_ref[...]) + sh_ref[...]
    hb = h.astype(_BF16)

    xa = _dot(hb, win_ref[:, 0:D_POOL])
    xa_ext[POOL_HIST:POOL_HIST + T, :] = xa

    def back(j, half):
        return xa_ext[pl.ds(POOL_HIST - j, T), half * LANES:(half + 1) * LANES]

    lane = jax.lax.broadcasted_iota(jnp.int32, (1, LANES), 1)
    lo = lane < POOL_GROUP
    pos1 = (t * T + 1 + jax.lax.broadcasted_iota(jnp.int32, (T, LANES), 0)).astype(_F32)
    means = []
    for half in range(2):
        w_a, w_b = POOL_WINDOWS[2 * half], POOL_WINDOWS[2 * half + 1]
        s = back(0, half)
        for j in range(1, w_a):
            s = s + back(j, half)
        s_a = s
        for j in range(w_a, w_b):
            s = s + back(j, half)
        s_b = s
        cnt = jnp.minimum(pos1, jnp.where(lo, float(w_a), float(w_b)))
        means.append(jnp.where(lo, s_a, s_b) / cnt)
    d = jnp.concatenate(means, axis=-1) - xa
    ya = _dot(d.astype(_BF16), poolw_ref[...]) * pools_ref[...]
    ycat[:, 0:D_POOL] = ya.astype(_BF16)

    a = _dot(hb, win_ref[:, D_POOL:D_POOL + D_CONV])
    gt = _dot(hb, win_ref[:, D_POOL + D_CONV:D_POOL + 2 * D_CONV])
    glu = a * _sigmoid(gt)
    glu_ext[CONV_HIST:CONV_HIST + T, :] = glu
    acc = convb_ref[...] + convw_ref[CONV_WIDTH - 1:CONV_WIDTH, :] * glu
    for k in range(CONV_WIDTH - 1):
        acc = acc + convw_ref[k:k + 1, :] * glu_ext[pl.ds(CONV_HIST - CONV_BUF + k, T), :]
    yb = _silu(_layer_norm(acc, clg_ref[...], clb_ref[...]))
    ycat[:, D_POOL:D_POOL + D_CONV] = yb.astype(_BF16)

    off_u = D_POOL + 2 * D_CONV
    u = _gelu_tanh(_dot(hb, win_ref[:, off_u:off_u + D_GMLP]))
    v = _gelu_tanh(_dot(hb, win_ref[:, off_u + D_GMLP:off_u + 2 * D_GMLP]))
    v = _layer_norm(v, glg_ref[...], glb_ref[...])
    zero = jnp.zeros((CHUNK, LANES), _F32)
    for cp in range(T // (2 * CHUNK)):
        r0 = cp * 2 * CHUNK
        for m in range(N_GMLP_HEADS // 2):
            l0 = m * LANES
            v0 = v[r0:r0 + CHUNK, l0:l0 + LANES]
            v1 = v[r0 + CHUNK:r0 + 2 * CHUNK, l0:l0 + LANES]
            top = jnp.concatenate([jnp.where(lo, v0, zero), jnp.where(lo, v1, zero)], axis=1)
            bot = jnp.concatenate([jnp.where(lo, zero, v0), jnp.where(lo, zero, v1)], axis=1)
            rhs = jnp.concatenate([top, bot], axis=0).astype(_BF16)
            z = _dot(wsp_ref[m], rhs)
            bias = bsf_ref[:, l0:l0 + LANES]
            y0 = u[r0:r0 + CHUNK, l0:l0 + LANES] * (z[:, 0:LANES] + bias)
            y1 = u[r0 + CHUNK:r0 + 2 * CHUNK, l0:l0 + LANES] * (z[:, LANES:2 * LANES] + bias)
            c0 = D_POOL + D_CONV + l0
            ycat[r0:r0 + CHUNK, c0:c0 + LANES] = y0.astype(_BF16)
            ycat[r0 + CHUNK:r0 + 2 * CHUNK, c0:c0 + LANES] = y1.astype(_BF16)

    y = _dot(ycat[...], wout_ref[...])
    o_ref[...] = x + g_ref[...] * y

    @pl.when(t == n_t - 1)
    def _():
        npool_ref[...] = xa_ext[pl.ds(POOL_HIST + T - POOL_BUF, POOL_BUF), :]
        nconv_ref[...] = glu_ext[pl.ds(CONV_HIST + T - CONV_BUF, CONV_BUF), :]

    xa_ext[0:POOL_HIST, :] = xa_ext[T:T + POOL_HIST, :]
    glu_ext[0:CONV_HIST, :] = glu_ext[T:T + CONV_HIST, :]


def _const_spec(shape):
    nd = len(shape)
    return pl.BlockSpec(shape, lambda *_: (0,) * nd, pipeline_mode=pl.Buffered(1))


def _mod_spec_prompt(layer, k, n_sample):
    return pl.BlockSpec((None, None, None, 1, D_MODEL),
                        lambda n, t: (layer, n_sample + n, k, 0, 0))


def _mixer_prompt_call(layer, x, mod5, n_sample, lw):
    nb, seq, _ = x.shape
    n_t = seq // TOKEN_TILE
    T = TOKEN_TILE
    tile = pl.BlockSpec((None, T, D_MODEL), lambda n, t: (n, t, 0))
    in_specs = [tile] + [_mod_spec_prompt(layer, k, n_sample) for k in range(3)] + [
        _const_spec((1, D_MODEL)),
        _const_spec((D_MODEL, D_IN)),
        _const_spec((D_POOL, D_POOL)),
        _const_spec((1, D_POOL)),
        _const_spec((CONV_WIDTH, D_CONV)),
        _const_spec((1, D_CONV)),
        _const_spec((1, D_CONV)),
        _const_spec((1, D_CONV)),
        _const_spec((1, D_GMLP)),
        _const_spec((1, D_GMLP)),
        _const_spec((N_GMLP_HEADS // 2, CHUNK, 2 * CHUNK)),
        _const_spec((CHUNK, D_GMLP)),
        _const_spec((D_MODEL, D_MODEL)),
    ]
    out_shape = (
        jax.ShapeDtypeStruct((nb, seq, D_MODEL), _F32),
        jax.ShapeDtypeStruct((nb, POOL_BUF, D_POOL), _F32),
        jax.ShapeDtypeStruct((nb, CONV_BUF, D_CONV), _F32),
    )
    out_specs = (
        tile,
        pl.BlockSpec((None, POOL_BUF, D_POOL), lambda n, t: (n, 0, 0)),
        pl.BlockSpec((None, CONV_BUF, D_CONV), lambda n, t: (n, 0, 0)),
    )
    return pl.pallas_call(
        _mixer_prompt_kernel,
        out_shape=out_shape,
        grid=(nb, n_t),
        in_specs=in_specs,
        out_specs=out_specs,
        scratch_shapes=[
            pltpu.VMEM((T + POOL_HIST, D_POOL), _F32),
            pltpu.VMEM((T + CONV_HIST, D_CONV), _F32),
            pltpu.VMEM((T, D_MODEL), _BF16),
        ],
        compiler_params=pltpu.CompilerParams(
            dimension_semantics=("arbitrary", "arbitrary"), vmem_limit_bytes=VMEM_LIMIT),
        name="mixer_prompt",
    )(x, mod5, mod5, mod5, lw["norm1_g"], lw["w_in"], lw["pool_bd"], lw["pool_scale"],
      lw["conv_w"], lw["conv_b"], lw["conv_ln_g"], lw["conv_ln_b"], lw["gmlp_ln_g"],
      lw["gmlp_ln_b"], lw["ws_pairs"], lw["bs_full"], lw["w_out"])


def _ffn_kernel(x_ref, sh_ref, sc_ref, g_ref, ng_ref, w1_ref, w3_ref, w2_ref, fg_ref,
                o_ref, act, *, final_norm):
    x = x_ref[...]
    h = _rms_norm(x, ng_ref[...]) * (1.0 + sc_ref[...]) + sh_ref[...]
    hb = h.astype(_BF16)
    for c in range(D_FF // FF_CHUNK):
        cols = slice(c * FF_CHUNK, (c + 1) * FF_CHUNK)
        a1 = _dot(hb, w1_ref[:, cols])
        a3 = _dot(hb, w3_ref[:, cols])
        act[:, cols] = (_silu(a1) * a3).astype(_BF16)
    ff = _dot(act[...], w2_ref[...])
    out = x + g_ref[...] * ff
    if final_norm:
        out = _rms_norm(out, fg_ref[...])
    o_ref[...] = out


def _ffn_weight_specs():
    return [
        _const_spec((1, D_MODEL)),
        _const_spec((D_MODEL, D_FF)),
        _const_spec((D_MODEL, D_FF)),
        _const_spec((D_FF, D_MODEL)),
        _const_spec((1, D_MODEL)),
    ]


def _ffn_prompt_call(layer, x, mod5, n_sample, lw, final_g, final_norm):
    nb, seq, _ = x.shape
    T = TOKEN_TILE
    tile = pl.BlockSpec((None, T, D_MODEL), lambda n, t: (n, t, 0))
    in_specs = [tile] + [_mod_spec_prompt(layer, k, n_sample) for k in range(3, 6)]
    in_specs += _ffn_weight_specs()
    return pl.pallas_call(
        functools.partial(_ffn_kernel, final_norm=final_norm),
        out_shape=jax.ShapeDtypeStruct((nb, seq, D_MODEL), _F32),
        grid=(nb, seq // T),
        in_specs=in_specs,
        out_specs=tile,
        scratch_shapes=[pltpu.VMEM((T, D_FF), _BF16)],
        compiler_params=pltpu.CompilerParams(
            dimension_semantics=("arbitrary", "arbitrary"), vmem_limit_bytes=VMEM_LIMIT),
        name="ffn_prompt",
    )(x, mod5, mod5, mod5, lw["norm2_g"], lw["w_ff1"], lw["w_ff3"], lw["w_ff2"], final_g)


def _mod_spec_sample(layer, k, n_sample):
    return pl.BlockSpec((None, n_sample, D_MODEL), lambda i: (layer, 0, k))


def _ffn_sample_call(layer, xs, mod, lw, final_g, final_norm):
    n = xs.shape[0]
    full = pl.BlockSpec((n, D_MODEL), lambda i: (0, 0))
    in_specs = [full] + [_mod_spec_sample(layer, k, n) for k in range(3, 6)]
    in_specs += _ffn_weight_specs()
    return pl.pallas_call(
        functools.partial(_ffn_kernel, final_norm=final_norm),
        out_shape=jax.ShapeDtypeStruct((n, D_MODEL), _F32),
        grid=(1,),
        in_specs=in_specs,
        out_specs=full,
        scratch_shapes=[pltpu.VMEM((n, D_FF), _BF16)],
        compiler_params=pltpu.CompilerParams(
            dimension_semantics=("arbitrary",), vmem_limit_bytes=VMEM_LIMIT),
        name="ffn_sample",
    )(xs, mod, mod, mod, lw["norm2_g"], lw["w_ff1"], lw["w_ff3"], lw["w_ff2"], final_g)


def _mixer_sample_kernel(x_ref, sh_ref, sc_ref, g_ref, ng_ref, win_ref, poolw_ref, pools_ref,
                         convw_ref, convb_ref, clg_ref, clb_ref, glg_ref, glb_ref, ws0_ref,
                         bs0_ref, wout_ref, spool_ref, sconv_ref,
                         o_ref, npool_ref, nconv_ref, vrow_ref):
    n = x_ref.shape[0]
    x = x_ref[...]
    h = _rms_norm(x, ng_ref[...]) * (1.0 + sc_ref[...]) + sh_ref[...]
    hb = h.astype(_BF16)

    def state_rows(ref, buf, r, half):
        return ref[pl.ds(2 * r + half, n, stride=2 * buf), :]

    xa = _dot(hb, win_ref[:, 0:D_POOL])
    lane = jax.lax.broadcasted_iota(jnp.int32, (1, LANES), 1)
    lo = lane < POOL_GROUP
    means = []
    for half in range(2):
        w_a, w_b = POOL_WINDOWS[2 * half], POOL_WINDOWS[2 * half + 1]
        xa_h = xa[:, half * LANES:(half + 1) * LANES]
        s = xa_h
        for j in range(1, w_a):
            s = s + state_rows(spool_ref, POOL_BUF, POOL_BUF - j, half)
        s_a = s
        for j in range(w_a, w_b):
            s = s + state_rows(spool_ref, POOL_BUF, POOL_BUF - j, half)
        s_b = s
        means.append(jnp.where(lo, s_a, s_b) / jnp.where(lo, float(w_a), float(w_b)))
        for r in range(POOL_BUF - 1):
            npool_ref[pl.ds(2 * r + half, n, stride=2 * POOL_BUF), :] = state_rows(
                spool_ref, POOL_BUF, r + 1, half)
        npool_ref[pl.ds(2 * (POOL_BUF - 1) + half, n, stride=2 * POOL_BUF), :] = xa_h
    d = jnp.concatenate(means, axis=-1) - xa
    ya = _dot(d.astype(_BF16), poolw_ref[...]) * pools_ref[...]

    a = _dot(hb, win_ref[:, D_POOL:D_POOL + D_CONV])
    gt = _dot(hb, win_ref[:, D_POOL + D_CONV:D_POOL + 2 * D_CONV])
    glu = a * _sigmoid(gt)
    accs = []
    for half in range(2):
        cols = slice(half * LANES, (half + 1) * LANES)
        glu_h = glu[:, cols]
        acc = convb_ref[:, cols] + convw_ref[CONV_WIDTH - 1:CONV_WIDTH, cols] * glu_h
        for r in range(CONV_BUF):
            row = state_rows(sconv_ref, CONV_BUF, r, half)
            acc = acc + convw_ref[r:r + 1, cols] * row
            if r >= 1:
                nconv_ref[pl.ds(2 * (r - 1) + half, n, stride=2 * CONV_BUF), :] = row
        nconv_ref[pl.ds(2 * (CONV_BUF - 1) + half, n, stride=2 * CONV_BUF), :] = glu_h
        accs.append(acc)
    yb = _silu(_layer_norm(jnp.concatenate(accs, axis=-1), clg_ref[...], clb_ref[...]))

    off_u = D_POOL + 2 * D_CONV
    u = _gelu_tanh(_dot(hb, win_ref[:, off_u:off_u + D_GMLP]))
    v = _gelu_tanh(_dot(hb, win_ref[:, off_u + D_GMLP:off_u + 2 * D_GMLP]))
    v = _layer_norm(v, glg_ref[...], glb_ref[...])
    vrow_ref[...] = v
    yc = u * (ws0_ref[...] * v + bs0_ref[...])

    ycat = jnp.concatenate([ya, yb, yc], axis=-1).astype(_BF16)
    y = _dot(ycat, wout_ref[...])
    o_ref[...] = x + g_ref[...] * y


def _mixer_sample_call(layer, xs, mod, lw, spool, sconv):
    n = xs.shape[0]

    def whole(shape):
        nd = len(shape)
        return pl.BlockSpec(shape, lambda i: (0,) * nd)

    in_specs = [whole((n, D_MODEL))] + [_mod_spec_sample(layer, k, n) for k in range(3)] + [
        whole((1, D_MODEL)),
        whole((D_MODEL, D_IN)),
        whole((D_POOL, D_POOL)),
        whole((1, D_POOL)),
        whole((CONV_WIDTH, D_CONV)),
        whole((1, D_CONV)),
        whole((1, D_CONV)),
        whole((1, D_CONV)),
        whole((1, D_GMLP)),
        whole((1, D_GMLP)),
        whole((1, D_GMLP)),
        whole((1, D_GMLP)),
        whole((D_MODEL, D_MODEL)),
        whole((n * POOL_BUF * 2, LANES)),
        whole((n * CONV_BUF * 2, LANES)),
    ]
    out_shape = (
        jax.ShapeDtypeStruct((n, D_MODEL), _F32),
        jax.ShapeDtypeStruct((n * POOL_BUF * 2, LANES), _F32),
        jax.ShapeDtypeStruct((n * CONV_BUF * 2, LANES), _F32),
        jax.ShapeDtypeStruct((n, D_GMLP), _F32),
    )
    out_specs = (
        whole((n, D_MODEL)),
        whole((n * POOL_BUF * 2, LANES)),
        whole((n * CONV_BUF * 2, LANES)),
        whole((n, D_GMLP)),
    )
    return pl.pallas_call(
        _mixer_sample_kernel,
        out_shape=out_shape,
        grid=(1,),
        in_specs=in_specs,
        out_specs=out_specs,
        compiler_params=pltpu.CompilerParams(
            dimension_semantics=("arbitrary",), vmem_limit_bytes=VMEM_LIMIT),
        name="mixer_sample",
    )(xs, mod, mod, mod, lw["norm1_g"], lw["w_in"], lw["pool_bd"], lw["pool_scale"],
      lw["conv_w"], lw["conv_b"], lw["conv_ln_g"], lw["conv_ln_b"], lw["gmlp_ln_g"],
      lw["gmlp_ln_b"], lw["ws0"], lw["bs0"], lw["w_out"], spool, sconv)


def _prep_layer(l, norm1_g, norm2_g, w_in, pool_w, pool_scale, conv_w, conv_b, conv_ln_g,
                conv_ln_b, gmlp_ln_g, gmlp_ln_b, gmlp_ws, gmlp_bs, w_out, w_ff1, w_ff3, w_ff2):
    row = lambda a: a[l].reshape(1, -1)
    eye = jnp.eye(len(POOL_WINDOWS), dtype=_F32)
    pool_bd = (eye[:, None, :, None] * pool_w[l][:, :, None, :]).reshape(D_POOL, D_POOL)
    mask = jnp.tril(jnp.ones((CHUNK, CHUNK), dtype=bool))
    ws = jnp.where(mask[None], gmlp_ws[l], jnp.zeros_like(gmlp_ws[l]))
    ws_pairs = ws.reshape(N_GMLP_HEADS // 2, 2, CHUNK, CHUNK).transpose(0, 2, 1, 3).reshape(
        N_GMLP_HEADS // 2, CHUNK, 2 * CHUNK)
    bs_full = jnp.repeat(gmlp_bs[l].T, GMLP_HEAD, axis=1)
    ws0 = jnp.repeat(gmlp_ws[l][:, 0, 0], GMLP_HEAD).reshape(1, D_GMLP)
    bs0 = jnp.repeat(gmlp_bs[l][:, 0], GMLP_HEAD).reshape(1, D_GMLP)
    return dict(
        norm1_g=row(norm1_g), norm2_g=row(norm2_g),
        w_in=w_in[l].astype(_BF16), pool_bd=pool_bd.astype(_BF16), pool_scale=row(pool_scale),
        conv_w=conv_w[l], conv_b=row(conv_b), conv_ln_g=row(conv_ln_g), conv_ln_b=row(conv_ln_b),
        gmlp_ln_g=row(gmlp_ln_g), gmlp_ln_b=row(gmlp_ln_b),
        ws_pairs=ws_pairs.astype(_BF16), bs_full=bs_full, ws0=ws0, bs0=bs0,
        w_out=w_out[l].astype(_BF16),
        w_ff1=w_ff1[l].astype(_BF16), w_ff3=w_ff3[l].astype(_BF16), w_ff2=w_ff2[l].astype(_BF16),
    )


def kernel(x_prompt, x_sample, c_prompt, c_sample, state_pool, state_conv, w_ada, b_ada, norm1_g, norm2_g, w_in, pool_w, pool_scale, conv_w, conv_b, conv_ln_g, conv_ln_b, gmlp_ln_g, gmlp_ln_b, gmlp_ws, gmlp_bs, w_out, w_ff1, w_ff3, w_ff2, final_g):
    depth = w_in.shape[0]
    nb = x_prompt.shape[0]
    ns = x_sample.shape[0]
    assert x_sample.shape[1] == 1 and x_prompt.shape[1] % TOKEN_TILE == 0

    c_all = jnp.concatenate([c_sample, c_prompt], axis=0)
    mod = _ada_call(c_all, w_ada, b_ada)
    mod5 = mod.reshape(depth, ns + nb, N_MOD, 1, D_MODEL)
    fg = final_g.reshape(1, D_MODEL)

    xp = x_prompt
    xs = x_sample.reshape(ns, D_MODEL)
    pool_p, conv_p, pool_s, conv_s, v_s = [], [], [], [], []
    for l in range(depth):
        lw = _prep_layer(l, norm1_g, norm2_g, w_in, pool_w, pool_scale, conv_w, conv_b, conv_ln_g,
                         conv_ln_b, gmlp_ln_g, gmlp_ln_b, gmlp_ws, gmlp_bs, w_out, w_ff1, w_ff3,
                         w_ff2)
        last = l == depth - 1
        xp, npool_p, nconv_p = _mixer_prompt_call(l, xp, mod5, ns, lw)
        xp = _ffn_prompt_call(l, xp, mod5, ns, lw, fg, last)
        spool = state_pool[l].reshape(ns * POOL_BUF * 2, LANES)
        sconv = state_conv[l].reshape(ns * CONV_BUF * 2, LANES)
        xs, npool_s, nconv_s, nv_s = _mixer_sample_call(l, xs, mod, lw, spool, sconv)
        xs = _ffn_sample_call(l, xs, mod, lw, fg, last)
        pool_p.append(npool_p)
        conv_p.append(nconv_p)
        pool_s.append(npool_s.reshape(ns, POOL_BUF, D_POOL))
        conv_s.append(nconv_s.reshape(ns, CONV_BUF, D_CONV))
        v_s.append(nv_s.reshape(ns, 1, D_GMLP))
    return (xp, xs.reshape(ns, 1, D_MODEL), jnp.stack(pool_p), jnp.stack(conv_p),
            jnp.stack(pool_s), jnp.stack(conv_s), jnp.stack(v_s))
```

```python
import functools

import jax
import jax.numpy as jnp
from jax.experimental import pallas as pl
from jax.experimental.pallas import tpu as pltpu

D_MODEL = 1024
D_POOL = 256
POOL_WINDOWS = (2, 4, 8, 16)
POOL_GROUP = 64
POOL_BUF = 15
D_CONV = 256
CONV_WIDTH = 31
CONV_BUF = 30
D_GMLP = 512
GMLP_HEAD = 64
N_GMLP_HEADS = 8
CHUNK = 128
D_IN = D_POOL + 2 * D_CONV + 2 * D_GMLP
D_FF = 2816
N_MOD = 6
EPS = 1e-6

LANES = 128
SUBLANES = 8

TOKEN_TILE = 512
FF_CHUNK = 256
ADA_COLS = 1536
POOL_HIST = 16
CONV_HIST = 32
VMEM_LIMIT = 56 * 1024 * 1024

_BF16 = jnp.bfloat16
_F32 = jnp.float32


def _dot(a, b):
    return jnp.dot(a, b, preferred_element_type=_F32)


def _sigmoid(x):
    return 1.0 / (1.0 + jnp.exp(-x))


def _silu(x):
    return x * _sigmoid(x)


def _gelu_tanh(x):
    c = 0.7978845608028654
    return 0.5 * x * (1.0 + jnp.tanh(c * (x + 0.044715 * (x * x * x))))


def _rms_norm(x, g):
    ms = jnp.mean(x * x, axis=-1, keepdims=True)
    return (x * jax.lax.rsqrt(ms + EPS)) * g


def _layer_norm(x, g, b):
    mu = jnp.mean(x, axis=-1, keepdims=True)
    xc = x - mu
    var = jnp.mean(xc * xc, axis=-1, keepdims=True)
    return xc * jax.lax.rsqrt(var + EPS) * g + b


def _ada_kernel(c_ref, w_ref, b_ref, o_ref):
    s = _silu(c_ref[...]).astype(_BF16)
    o_ref[...] = _dot(s, w_ref[...].astype(_BF16)) + b_ref[...]


def _ada_call(c_all, w_ada, b_ada):
    depth = w_ada.shape[0]
    n = c_all.shape[0]
    n_col = (N_MOD * D_MODEL) // ADA_COLS
    return pl.pallas_call(
        _ada_kernel,
        out_shape=jax.ShapeDtypeStruct((depth, n, N_MOD * D_MODEL), _F32),
        grid=(depth, n_col),
        in_specs=[
            pl.BlockSpec((n, D_MODEL), lambda l, j: (0, 0)),
            pl.BlockSpec((None, D_MODEL, ADA_COLS), lambda l, j: (l, 0, j)),
            pl.BlockSpec((None, 1, ADA_COLS), lambda l, j: (l, 0, j)),
        ],
        out_specs=pl.BlockSpec((None, n, ADA_COLS), lambda l, j: (l, 0, j)),
        compiler_params=pltpu.CompilerParams(
            dimension_semantics=("arbitrary", "arbitrary"), vmem_limit_bytes=VMEM_LIMIT),
        name="ada_mod",
    )(c_all, w_ada, b_ada.reshape(depth, 1, N_MOD * D_MODEL))


def _mixer_prompt_kernel(x_ref, sh_ref, sc_ref, g_ref, ng_ref, win_ref, poolw_ref, pools_ref,
                         convw_ref, convb_ref, clg_ref, clb_ref, glg_ref, glb_ref, wsp_ref,
                         bsf_ref, wout_ref,
                         o_ref, npool_ref, nconv_ref,
                         xa_ext, glu_ext, ycat):
    t = pl.program_id(1)
    n_t = pl.num_programs(1)
    T = TOKEN_TILE

    @pl.when(t == 0)
    def _():
        xa_ext[0:POOL_HIST, :] = jnp.zeros((POOL_HIST, D_POOL), _F32)
        glu_ext[0:CONV_HIST, :] = jnp.zeros((CONV_HIST, D_CONV), _F32)
        glu_ext[CONV_HIST + T:CONV_HIST + T + SUBLANES, :] = jnp.zeros((SUBLANES, D_CONV), _F32)

    x = x_ref[...]
    h = _rms_norm(x, ng_ref[...]) * (1.0 + sc_ref[...]) + sh_ref[...]
    hb = h.astype(_BF16)

    xa = _dot(hb, win_ref[:, 0:D_POOL])
    xa_ext[POOL_HIST:POOL_HIST + T, :] = xa

    lane = jax.lax.broadcasted_iota(jnp.int32, (1, LANES), 1)
    lo = lane < POOL_GROUP
    pos1 = (t * T + 1 + jax.lax.broadcasted_iota(jnp.int32, (T, LANES), 0)).astype(_F32)
    means = []
    for half in range(2):
        w_a, w_b = POOL_WINDOWS[2 * half], POOL_WINDOWS[2 * half + 1]
        cur = xa_ext[:, half * LANES:(half + 1) * LANES]
        sums = {1: cur}
        w = 1
        while w < w_b:
            cur = cur + pltpu.roll(cur, w, axis=0)
            w *= 2
            sums[w] = cur
        s_a = sums[w_a][POOL_HIST:POOL_HIST + T, :]
        s_b = sums[w_b][POOL_HIST:POOL_HIST + T, :]
        cnt = jnp.minimum(pos1, jnp.where(lo, float(w_a), float(w_b)))
        means.append(jnp.where(lo, s_a, s_b) / cnt)
    d = jnp.concatenate(means, axis=-1) - xa
    ya = _dot(d.astype(_BF16), poolw_ref[...]) * pools_ref[...]
    ycat[:, 0:D_POOL] = ya.astype(_BF16)

    a = _dot(hb, win_ref[:, D_POOL:D_POOL + D_CONV])
    gt = _dot(hb, win_ref[:, D_POOL + D_CONV:D_POOL + 2 * D_CONV])
    glu = a * _sigmoid(gt)
    glu_ext[CONV_HIST:CONV_HIST + T, :] = glu
    lead = CONV_HIST - CONV_BUF
    acc = convb_ref[...]
    for r in range(SUBLANES):
        part = None
        for q in range((lead + CONV_WIDTH - 1) // SUBLANES + 1):
            j = SUBLANES * q + r
            if lead <= j < lead + CONV_WIDTH:
                term = convw_ref[j - lead:j - lead + 1, :] * glu_ext[SUBLANES * q:SUBLANES * q + T + SUBLANES, :]
                part = term if part is None else part + term
        acc = acc + part[r:r + T, :]
    yb = _silu(_layer_norm(acc, clg_ref[...], clb_ref[...]))
    ycat[:, D_POOL:D_POOL + D_CONV] = yb.astype(_BF16)

    off_u = D_POOL + 2 * D_CONV
    u = _gelu_tanh(_dot(hb, win_ref[:, off_u:off_u + D_GMLP]))
    v = _gelu_tanh(_dot(hb, win_ref[:, off_u + D_GMLP:off_u + 2 * D_GMLP]))
    v = _layer_norm(v, glg_ref[...], glb_ref[...])
    zero = jnp.zeros((CHUNK, LANES), _F32)
    for cp in range(T // (2 * CHUNK)):
        r0 = cp * 2 * CHUNK
        for m in range(N_GMLP_HEADS // 2):
            l0 = m * LANES
            v0 = v[r0:r0 + CHUNK, l0:l0 + LANES]
            v1 = v[r0 + CHUNK:r0 + 2 * CHUNK, l0:l0 + LANES]
            top = jnp.concatenate([jnp.where(lo, v0, zero), jnp.where(lo, v1, zero)], axis=1)
            bot = jnp.concatenate([jnp.where(lo, zero, v0), jnp.where(lo, zero, v1)], axis=1)
            rhs = jnp.concatenate([top, bot], axis=0).astype(_BF16)
            z = _dot(wsp_ref[m], rhs)
            bias = bsf_ref[:, l0:l0 + LANES]
            y0 = u[r0:r0 + CHUNK, l0:l0 + LANES] * (z[:, 0:LANES] + bias)
            y1 = u[r0 + CHUNK:r0 + 2 * CHUNK, l0:l0 + LANES] * (z[:, LANES:2 * LANES] + bias)
            c0 = D_POOL + D_CONV + l0
            ycat[r0:r0 + CHUNK, c0:c0 + LANES] = y0.astype(_BF16)
            ycat[r0 + CHUNK:r0 + 2 * CHUNK, c0:c0 + LANES] = y1.astype(_BF16)

    y = _dot(ycat[...], wout_ref[...])
    o_ref[...] = x + g_ref[...] * y

    @pl.when(t == n_t - 1)
    def _():
        npool_ref[...] = xa_ext[pl.ds(POOL_HIST + T - POOL_BUF, POOL_BUF), :]
        nconv_ref[...] = glu_ext[pl.ds(CONV_HIST + T - CONV_BUF, CONV_BUF), :]

    xa_ext[0:POOL_HIST, :] = xa_ext[T:T + POOL_HIST, :]
    glu_ext[0:CONV_HIST, :] = glu_ext[T:T + CONV_HIST, :]


def _const_spec(shape):
    nd = len(shape)
    return pl.BlockSpec(shape, lambda *_: (0,) * nd, pipeline_mode=pl.Buffered(1))


def _mod_spec_prompt(layer, k, n_sample):
    return pl.BlockSpec((None, None, None, 1, D_MODEL),
                        lambda n, t: (layer, n_sample + n, k, 0, 0))


def _mixer_prompt_call(layer, x, mod5, n_sample, lw):
    nb, seq, _ = x.shape
    n_t = seq // TOKEN_TILE
    T = TOKEN_TILE
    tile = pl.BlockSpec((None, T, D_MODEL), lambda n, t: (n, t, 0))
    in_specs = [tile] + [_mod_spec_prompt(layer, k, n_sample) for k in range(3)] + [
        _const_spec((1, D_MODEL)),
        _const_spec((D_MODEL, D_IN)),
        _const_spec((D_POOL, D_POOL)),
        _const_spec((1, D_POOL)),
        _const_spec((CONV_WIDTH, D_CONV)),
        _const_spec((1, D_CONV)),
        _const_spec((1, D_CONV)),
        _const_spec((1, D_CONV)),
        _const_spec((1, D_GMLP)),
        _const_spec((1, D_GMLP)),
        _const_spec((N_GMLP_HEADS // 2, CHUNK, 2 * CHUNK)),
        _const_spec((CHUNK, D_GMLP)),
        _const_spec((D_MODEL, D_MODEL)),
    ]
    out_shape = (
        jax.ShapeDtypeStruct((nb, seq, D_MODEL), _F32),
        jax.ShapeDtypeStruct((nb, POOL_BUF, D_POOL), _F32),
        jax.ShapeDtypeStruct((nb, CONV_BUF, D_CONV), _F32),
    )
    out_specs = (
        tile,
        pl.BlockSpec((None, POOL_BUF, D_POOL), lambda n, t: (n, 0, 0)),
        pl.BlockSpec((None, CONV_BUF, D_CONV), lambda n, t: (n, 0, 0)),
    )
    return pl.pallas_call(
        _mixer_prompt_kernel,
        out_shape=out_shape,
        grid=(nb, n_t),
        in_specs=in_specs,
        out_specs=out_specs,
        scratch_shapes=[
            pltpu.VMEM((T + POOL_HIST, D_POOL), _F32),
            pltpu.VMEM((T + CONV_HIST + SUBLANES, D_CONV), _F32),
            pltpu.VMEM((T, D_MODEL), _BF16),
        ],
        compiler_params=pltpu.CompilerParams(
            dimension_semantics=("arbitrary", "arbitrary"), vmem_limit_bytes=VMEM_LIMIT),
        name="mixer_prompt",
    )(x, mod5, mod5, mod5, lw["norm1_g"], lw["w_in"], lw["pool_bd"], lw["pool_scale"],
      lw["conv_w"], lw["conv_b"], lw["conv_ln_g"], lw["conv_ln_b"], lw["gmlp_ln_g"],
      lw["gmlp_ln_b"], lw["ws_pairs"], lw["bs_full"], lw["w_out"])


def _ffn_kernel(x_ref, sh_ref, sc_ref, g_ref, ng_ref, w1_ref, w3_ref, w2_ref, fg_ref,
                o_ref, act, *, final_norm):
    x = x_ref[...]
    h = _rms_norm(x, ng_ref[...]) * (1.0 + sc_ref[...]) + sh_ref[...]
    hb = h.astype(_BF16)
    for c in range(D_FF // FF_CHUNK):
        cols = slice(c * FF_CHUNK, (c + 1) * FF_CHUNK)
        a1 = _dot(hb, w1_ref[:, cols])
        a3 = _dot(hb, w3_ref[:, cols])
        act[:, cols] = (_silu(a1) * a3).astype(_BF16)
    ff = _dot(act[...], w2_ref[...])
    out = x + g_ref[...] * ff
    if final_norm:
        out = _rms_norm(out, fg_ref[...])
    o_ref[...] = out


def _ffn_weight_specs():
    return [
        _const_spec((1, D_MODEL)),
        _const_spec((D_MODEL, D_FF)),
        _const_spec((D_MODEL, D_FF)),
        _const_spec((D_FF, D_MODEL)),
        _const_spec((1, D_MODEL)),
    ]


def _ffn_prompt_call(layer, x, mod5, n_sample, lw, final_g, final_norm):
    nb, seq, _ = x.shape
    T = TOKEN_TILE
    tile = pl.BlockSpec((None, T, D_MODEL), lambda n, t: (n, t, 0))
    in_specs = [tile] + [_mod_spec_prompt(layer, k, n_sample) for k in range(3, 6)]
    in_specs += _ffn_weight_specs()
    return pl.pallas_call(
        functools.partial(_ffn_kernel, final_norm=final_norm),
        out_shape=jax.ShapeDtypeStruct((nb, seq, D_MODEL), _F32),
        grid=(nb, seq // T),
        in_specs=in_specs,
        out_specs=tile,
        scratch_shapes=[pltpu.VMEM((T, D_FF), _BF16)],
        compiler_params=pltpu.CompilerParams(
            dimension_semantics=("arbitrary", "arbitrary"), vmem_limit_bytes=VMEM_LIMIT),
        name="ffn_prompt",
    )(x, mod5, mod5, mod5, lw["norm2_g"], lw["w_ff1"], lw["w_ff3"], lw["w_ff2"], final_g)


def _mod_spec_sample(layer, k, n_sample):
    return pl.BlockSpec((None, n_sample, D_MODEL), lambda i: (layer, 0, k))


def _ffn_sample_call(layer, xs, mod, lw, final_g, final_norm):
    n = xs.shape[0]
    full = pl.BlockSpec((n, D_MODEL), lambda i: (0, 0))
    in_specs = [full] + [_mod_spec_sample(layer, k, n) for k in range(3, 6)]
    in_specs += _ffn_weight_specs()
    return pl.pallas_call(
        functools.partial(_ffn_kernel, final_norm=final_norm),
        out_shape=jax.ShapeDtypeStruct((n, D_MODEL), _F32),
        grid=(1,),
        in_specs=in_specs,
        out_specs=full,
        scratch_shapes=[pltpu.VMEM((n, D_FF), _BF16)],
        compiler_params=pltpu.CompilerParams(
            dimension_semantics=("arbitrary",), vmem_limit_bytes=VMEM_LIMIT),
        name="ffn_sample",
    )(xs, mod, mod, mod, lw["norm2_g"], lw["w_ff1"], lw["w_ff3"], lw["w_ff2"], final_g)


def _mixer_sample_kernel(x_ref, sh_ref, sc_ref, g_ref, ng_ref, win_ref, poolw_ref, pools_ref,
                         convw_ref, convb_ref, clg_ref, clb_ref, glg_ref, glb_ref, ws0_ref,
                         bs0_ref, wout_ref, spool_ref, sconv_ref,
                         o_ref, npool_ref, nconv_ref, vrow_ref):
    n = x_ref.shape[0]
    x = x_ref[...]
    h = _rms_norm(x, ng_ref[...]) * (1.0 + sc_ref[...]) + sh_ref[...]
    hb = h.astype(_BF16)

    def state_rows(ref, buf, r, half):
        return ref[pl.ds(2 * r + half, n, stride=2 * buf), :]

    xa = _dot(hb, win_ref[:, 0:D_POOL])
    lane = jax.lax.broadcasted_iota(jnp.int32, (1, LANES), 1)
    lo = lane < POOL_GROUP
    means = []
    for half in range(2):
        w_a, w_b = POOL_WINDOWS[2 * half], POOL_WINDOWS[2 * half + 1]
        xa_h = xa[:, half * LANES:(half + 1) * LANES]
        s = xa_h
        for j in range(1, w_a):
            s = s + state_rows(spool_ref, POOL_BUF, POOL_BUF - j, half)
        s_a = s
        for j in range(w_a, w_b):
            s = s + state_rows(spool_ref, POOL_BUF, POOL_BUF - j, half)
        s_b = s
        means.append(jnp.where(lo, s_a, s_b) / jnp.where(lo, float(w_a), float(w_b)))
        for r in range(POOL_BUF - 1):
            npool_ref[pl.ds(2 * r + half, n, stride=2 * POOL_BUF), :] = state_rows(
                spool_ref, POOL_BUF, r + 1, half)
        npool_ref[pl.ds(2 * (POOL_BUF - 1) + half, n, stride=2 * POOL_BUF), :] = xa_h
    d = jnp.concatenate(means, axis=-1) - xa
    ya = _dot(d.astype(_BF16), poolw_ref[...]) * pools_ref[...]

    a = _dot(hb, win_ref[:, D_POOL:D_POOL + D_CONV])
    gt = _dot(hb, win_ref[:, D_POOL + D_CONV:D_POOL + 2 * D_CONV])
    glu = a * _sigmoid(gt)
    accs = []
    for half in range(2):
        cols = slice(half * LANES, (half + 1) * LANES)
        glu_h = glu[:, cols]
        acc = convb_ref[:, cols] + convw_ref[CONV_WIDTH - 1:CONV_WIDTH, cols] * glu_h
        for r in range(CONV_BUF):
            row = state_rows(sconv_ref, CONV_BUF, r, half)
            acc = acc + convw_ref[r:r + 1, cols] * row
            if r >= 1:
                nconv_ref[pl.ds(2 * (r - 1) + half, n, stride=2 * CONV_BUF), :] = row
        nconv_ref[pl.ds(2 * (CONV_BUF - 1) + half, n, stride=2 * CONV_BUF), :] = glu_h
        accs.append(acc)
    yb = _silu(_layer_norm(jnp.concatenate(accs, axis=-1), clg_ref[...], clb_ref[...]))

    off_u = D_POOL + 2 * D_CONV
    u = _gelu_tanh(_dot(hb, win_ref[:, off_u:off_u + D_GMLP]))
    v = _gelu_tanh(_dot(hb, win_ref[:, off_u + D_GMLP:off_u + 2 * D_GMLP]))
    v = _layer_norm(v, glg_ref[...], glb_ref[...])
    vrow_ref[...] = v
    yc = u * (ws0_ref[...] * v + bs0_ref[...])

    ycat = jnp.concatenate([ya, yb, yc], axis=-1).astype(_BF16)
    y = _dot(ycat, wout_ref[...])
    o_ref[...] = x + g_ref[...] * y


def _mixer_sample_call(layer, xs, mod, lw, spool, sconv):
    n = xs.shape[0]

    def whole(shape):
        nd = len(shape)
        return pl.BlockSpec(shape, lambda i: (0,) * nd)

    in_specs = [whole((n, D_MODEL))] + [_mod_spec_sample(layer, k, n) for k in range(3)] + [
        whole((1, D_MODEL)),
        whole((D_MODEL, D_IN)),
        whole((D_POOL, D_POOL)),
        whole((1, D_POOL)),
        whole((CONV_WIDTH, D_CONV)),
        whole((1, D_CONV)),
        whole((1, D_CONV)),
        whole((1, D_CONV)),
        whole((1, D_GMLP)),
        whole((1, D_GMLP)),
        whole((1, D_GMLP)),
        whole((1, D_GMLP)),
        whole((D_MODEL, D_MODEL)),
        whole((n * POOL_BUF * 2, LANES)),
        whole((n * CONV_BUF * 2, LANES)),
    ]
    out_shape = (
        jax.ShapeDtypeStruct((n, D_MODEL), _F32),
        jax.ShapeDtypeStruct((n * POOL_BUF * 2, LANES), _F32),
        jax.ShapeDtypeStruct((n * CONV_BUF * 2, LANES), _F32),
        jax.ShapeDtypeStruct((n, D_GMLP), _F32),
    )
    out_specs = (
        whole((n, D_MODEL)),
        whole((n * POOL_BUF * 2, LANES)),
        whole((n * CONV_BUF * 2, LANES)),
        whole((n, D_GMLP)),
    )
    return pl.pallas_call(
        _mixer_sample_kernel,
        out_shape=out_shape,
        grid=(1,),
        in_specs=in_specs,
        out_specs=out_specs,
        compiler_params=pltpu.CompilerParams(
            dimension_semantics=("arbitrary",), vmem_limit_bytes=VMEM_LIMIT),
        name="mixer_sample",
    )(xs, mod, mod, mod, lw["norm1_g"], lw["w_in"], lw["pool_bd"], lw["pool_scale"],
      lw["conv_w"], lw["conv_b"], lw["conv_ln_g"], lw["conv_ln_b"], lw["gmlp_ln_g"],
      lw["gmlp_ln_b"], lw["ws0"], lw["bs0"], lw["w_out"], spool, sconv)


def _prep_layer(l, norm1_g, norm2_g, w_in, pool_w, pool_scale, conv_w, conv_b, conv_ln_g,
                conv_ln_b, gmlp_ln_g, gmlp_ln_b, gmlp_ws, gmlp_bs, w_out, w_ff1, w_ff3, w_ff2):
    row = lambda a: a[l].reshape(1, -1)
    eye = jnp.eye(len(POOL_WINDOWS), dtype=_F32)
    pool_bd = (eye[:, None, :, None] * pool_w[l][:, :, None, :]).reshape(D_POOL, D_POOL)
    mask = jnp.tril(jnp.ones((CHUNK, CHUNK), dtype=bool))
    ws = jnp.where(mask[None], gmlp_ws[l], jnp.zeros_like(gmlp_ws[l]))
    ws_pairs = ws.reshape(N_GMLP_HEADS // 2, 2, CHUNK, CHUNK).transpose(0, 2, 1, 3).reshape(
        N_GMLP_HEADS // 2, CHUNK, 2 * CHUNK)
    bs_full = jnp.repeat(gmlp_bs[l].T, GMLP_HEAD, axis=1)
    ws0 = jnp.repeat(gmlp_ws[l][:, 0, 0], GMLP_HEAD).reshape(1, D_GMLP)
    bs0 = jnp.repeat(gmlp_bs[l][:, 0], GMLP_HEAD).reshape(1, D_GMLP)
    return dict(
        norm1_g=row(norm1_g), norm2_g=row(norm2_g),
        w_in=w_in[l].astype(_BF16), pool_bd=pool_bd.astype(_BF16), pool_scale=row(pool_scale),
        conv_w=conv_w[l], conv_b=row(conv_b), conv_ln_g=row(conv_ln_g), conv_ln_b=row(conv_ln_b),
        gmlp_ln_g=row(gmlp_ln_g), gmlp_ln_b=row(gmlp_ln_b),
        ws_pairs=ws_pairs.astype(_BF16), bs_full=bs_full, ws0=ws0, bs0=bs0,
        w_out=w_out[l].astype(_BF16),
        w_ff1=w_ff1[l].astype(_BF16), w_ff3=w_ff3[l].astype(_BF16), w_ff2=w_ff2[l].astype(_BF16),
    )


def kernel(x_prompt, x_sample, c_prompt, c_sample, state_pool, state_conv, w_ada, b_ada, norm1_g, norm2_g, w_in, pool_w, pool_scale, conv_w, conv_b, conv_ln_g, conv_ln_b, gmlp_ln_g, gmlp_ln_b, gmlp_ws, gmlp_bs, w_out, w_ff1, w_ff3, w_ff2, final_g):
    depth = w_in.shape[0]
    nb = x_prompt.shape[0]
    ns = x_sample.shape[0]
    assert x_sample.shape[1] == 1 and x_prompt.shape[1] % TOKEN_TILE == 0

    c_all = jnp.concatenate([c_sample, c_prompt], axis=0)
    mod = _ada_call(c_all, w_ada, b_ada)
    mod5 = mod.reshape(depth, ns + nb, N_MOD, 1, D_MODEL)
    fg = final_g.reshape(1, D_MODEL)

    xp = x_prompt
    xs = x_sample.reshape(ns, D_MODEL)
    pool_p, conv_p, pool_s, conv_s, v_s = [], [], [], [], []
    for l in range(depth):
        lw = _prep_layer(l, norm1_g, norm2_g, w_in, pool_w, pool_scale, conv_w, conv_b, conv_ln_g,
                         conv_ln_b, gmlp_ln_g, gmlp_ln_b, gmlp_ws, gmlp_bs, w_out, w_ff1, w_ff3,
                         w_ff2)
        last = l == depth - 1
        xp, npool_p, nconv_p = _mixer_prompt_call(l, xp, mod5, ns, lw)
        xp = _ffn_prompt_call(l, xp, mod5, ns, lw, fg, last)
        spool = state_pool[l].reshape(ns * POOL_BUF * 2, LANES)
        sconv = state_conv[l].reshape(ns * CONV_BUF * 2, LANES)
        xs, npool_s, nconv_s, nv_s = _mixer_sample_call(l, xs, mod, lw, spool, sconv)
        xs = _ffn_sample_call(l, xs, mod, lw, fg, last)
        pool_p.append(npool_p)
        conv_p.append(nconv_p)
        pool_s.append(npool_s.reshape(ns, POOL_BUF, D_POOL))
        conv_s.append(nconv_s.reshape(ns, CONV_BUF, D_CONV))
        v_s.append(nv_s.reshape(ns, 1, D_GMLP))
    return (xp, xs.reshape(ns, 1, D_MODEL), jnp.stack(pool_p), jnp.stack(conv_p),
            jnp.stack(pool_s), jnp.stack(conv_s), jnp.stack(v_s))
```

```python
import functools

import jax
import jax.numpy as jnp
from jax.experimental import pallas as pl
from jax.experimental.pallas import tpu as pltpu

D_MODEL = 1024
D_POOL = 256
POOL_WINDOWS = (2, 4, 8, 16)
POOL_GROUP = 64
POOL_BUF = 15
D_CONV = 256
CONV_WIDTH = 31
CONV_BUF = 30
D_GMLP = 512
GMLP_HEAD = 64
N_GMLP_HEADS = 8
CHUNK = 128
D_IN = D_POOL + 2 * D_CONV + 2 * D_GMLP
D_FF = 2816
N_MOD = 6
EPS = 1e-6

LANES = 128
SUBLANES = 8

TOKEN_TILE = 512
FF_CHUNK = 256
POOL_HIST = 16
CONV_HIST = 32
VMEM_LIMIT = 60 * 1024 * 1024

_BF16 = jnp.bfloat16
_F32 = jnp.float32


def _dot(a, b):
    return jnp.dot(a, b, preferred_element_type=_F32)


def _sigmoid(x):
    return 1.0 / (1.0 + jnp.exp(-x))


def _silu(x):
    return x * _sigmoid(x)


def _gelu_tanh(x):
    c = 0.7978845608028654
    return 0.5 * x * (1.0 + jnp.tanh(c * (x + 0.044715 * (x * x * x))))


def _rms_norm(x, g):
    ms = jnp.mean(x * x, axis=-1, keepdims=True)
    return (x * jax.lax.rsqrt(ms + EPS)) * g


def _layer_norm(x, g, b):
    mu = jnp.mean(x, axis=-1, keepdims=True)
    xc = x - mu
    var = jnp.mean(xc * xc, axis=-1, keepdims=True)
    return xc * jax.lax.rsqrt(var + EPS) * g + b


def _lo_lanes():
    return jax.lax.broadcasted_iota(jnp.int32, (1, LANES), 1) < POOL_GROUP


def _ada_kernel(c_ref, w_ref, b_ref, o_ref):
    l = pl.program_id(0)
    s = _silu(c_ref[...]).astype(_BF16)
    o_ref[...] = _dot(s, w_ref[...].astype(_BF16)) + b_ref[pl.ds(l, 1), :]


def _ada_call(c_all, w_ada, b_ada):
    depth = w_ada.shape[0]
    n = c_all.shape[0]
    return pl.pallas_call(
        _ada_kernel,
        out_shape=jax.ShapeDtypeStruct((depth, N_MOD, n, D_MODEL), _F32),
        grid=(depth, N_MOD),
        in_specs=[
            pl.BlockSpec((n, D_MODEL), lambda l, j: (0, 0)),
            pl.BlockSpec((None, D_MODEL, D_MODEL), lambda l, j: (l, 0, j)),
            pl.BlockSpec((depth, D_MODEL), lambda l, j: (0, j)),
        ],
        out_specs=pl.BlockSpec((None, None, n, D_MODEL), lambda l, j: (l, j, 0, 0)),
        compiler_params=pltpu.CompilerParams(
            dimension_semantics=("arbitrary", "arbitrary"), vmem_limit_bytes=VMEM_LIMIT),
        name="ada_mod",
    )(c_all, w_ada, b_ada)


def _pool_mix(means, xa, poolw_ref, pools):
    d = jnp.concatenate(means, axis=-1) - xa
    return _dot(d.astype(_BF16), poolw_ref[...]) * pools


def _ffn_body(x, sh, sc, g, ng, w1_ref, w3_ref, w2_ref, fg, act, final_norm):
    h = _rms_norm(x, ng) * (1.0 + sc) + sh
    hb = h.astype(_BF16)
    for c in range(D_FF // FF_CHUNK):
        cols = slice(c * FF_CHUNK, (c + 1) * FF_CHUNK)
        a1 = _dot(hb, w1_ref[:, cols])
        a3 = _dot(hb, w3_ref[:, cols])
        act[:, cols] = (_silu(a1) * a3).astype(_BF16)
    ff = _dot(act[...], w2_ref[...])
    out = x + g * ff
    if final_norm:
        out = _rms_norm(out, fg)
    return out


def _mixer_prompt_stages(x_ref, sh, sc, g, ng, t, win_ref, poolw_ref, pools, convw_ref, convb, clg, clb,
                         glg, glb, wsp_ref, bsf_ref, wout_ref, xa_ext, glu_ext, ycat, finish):
    T = TOKEN_TILE
    x = x_ref[...]
    h = _rms_norm(x, ng) * (1.0 + sc) + sh
    hb = h.astype(_BF16)
    lo = _lo_lanes()
    yield

    xa = _dot(hb, win_ref[:, 0:D_POOL])
    a = _dot(hb, win_ref[:, D_POOL:D_POOL + D_CONV])
    gt = _dot(hb, win_ref[:, D_POOL + D_CONV:D_POOL + 2 * D_CONV])
    yield
    xa_ext[POOL_HIST:POOL_HIST + T, :] = xa
    glu_ext[CONV_HIST:CONV_HIST + T, :] = a * _sigmoid(gt)
    pos1 =(t * T + 1 + jax.lax.broadcasted_iota(jnp.int32, (T, LANES), 0)).astype(_F32)
    means = []
    for half in range(2):
        w_a, w_b = POOL_WINDOWS[2 * half], POOL_WINDOWS[2 * half + 1]
        cur = xa_ext[:, half * LANES:(half + 1) * LANES]
        sums = {1: cur}
        w = 1
        while w < w_b:
            cur = cur + pltpu.roll(cur, w, axis=0)
            w *= 2
            sums[w] = cur
        s_a = sums[w_a][POOL_HIST:POOL_HIST + T, :]
        s_b = sums[w_b][POOL_HIST:POOL_HIST + T, :]
        cnt = jnp.minimum(pos1, jnp.where(lo, float(w_a), float(w_b)))
        means.append(jnp.where(lo, s_a, s_b) / cnt)
    ya = _pool_mix(means, xa, poolw_ref, pools)
    ycat[:, 0:D_POOL] = ya.astype(_BF16)
    yield

    off_u = D_POOL + 2 * D_CONV
    u = _gelu_tanh(_dot(hb, win_ref[:, off_u:off_u + D_GMLP]))
    v = _gelu_tanh(_dot(hb, win_ref[:, off_u + D_GMLP:off_u + 2 * D_GMLP]))
    v = _layer_norm(v, glg, glb)
    yield

    lead = CONV_HIST - CONV_BUF
    acc = convb
    for r in range(SUBLANES):
        part = None
        for q in range((lead + CONV_WIDTH - 1) // SUBLANES + 1):
            j = SUBLANES * q + r
            if lead <= j < lead + CONV_WIDTH:
                term = convw_ref[j - lead:j - lead + 1, :] * glu_ext[SUBLANES * q:SUBLANES * q + T + SUBLANES, :]
                part = term if part is None else part + term
        acc = acc + part[r:r + T, :]
    yb = _silu(_layer_norm(acc, clg, clb))
    ycat[:, D_POOL:D_POOL + D_CONV] = yb.astype(_BF16)
    yield

    zero = jnp.zeros((CHUNK, LANES), _F32)
    for cp in range(T // (2 * CHUNK)):
        r0 = cp * 2 * CHUNK
        for m in range(N_GMLP_HEADS // 2):
            l0 = m * LANES
            v0 = v[r0:r0 + CHUNK, l0:l0 + LANES]
            v1 = v[r0 + CHUNK:r0 + 2 * CHUNK, l0:l0 + LANES]
            top = jnp.concatenate([jnp.where(lo, v0, zero), jnp.where(lo, v1, zero)], axis=1)
            bot = jnp.concatenate([jnp.where(lo, zero, v0), jnp.where(lo, zero, v1)], axis=1)
            rhs = jnp.concatenate([top, bot], axis=0).astype(_BF16)
            z = _dot(wsp_ref[m], rhs)
            bias = bsf_ref[:, l0:l0 + LANES]
            y0 = u[r0:r0 + CHUNK, l0:l0 + LANES] * (z[:, 0:LANES] + bias)
            y1 = u[r0 + CHUNK:r0 + 2 * CHUNK, l0:l0 + LANES] * (z[:, LANES:2 * LANES] + bias)
            c0 = D_POOL + D_CONV + l0
            ycat[r0:r0 + CHUNK, c0:c0 + LANES] = y0.astype(_BF16)
            ycat[r0 + CHUNK:r0 + 2 * CHUNK, c0:c0 + LANES] = y1.astype(_BF16)
    yield

    y = _dot(ycat[...], wout_ref[...])
    finish(x + g * y)
    xa_ext[0:POOL_HIST, :] = xa_ext[T:T + POOL_HIST, :]
    glu_ext[0:CONV_HIST, :] = glu_ext[T:T + CONV_HIST, :]
    yield


def _ffn_prompt_stages(o_ref, x1_buf, h2b, g, w1_ref, w3_ref, w2_ref, fg, act, final_norm):
    o_ref[...] = x1_buf[...]
    yield
    for c in range(D_FF // FF_CHUNK):
        cols = slice(c * FF_CHUNK, (c + 1) * FF_CHUNK)
        a1 = _dot(h2b[...], w1_ref[:, cols])
        a3 = _dot(h2b[...], w3_ref[:, cols])
        act[:, cols] = (_silu(a1) * a3).astype(_BF16)
        yield
    out = o_ref[...] + g * _dot(act[...], w2_ref[...])
    if final_norm:
        out = _rms_norm(out, fg)
    o_ref[...] = out
    yield


_STAGE_ORDER = "ba" + "bba" + "bbaa" + "bbbbb" + "aa" + "bba" + "b"


def _layer_prompt_kernel(x_ref, mod_ref, ng1_ref, ng2_ref, fg_ref, win_ref, poolw_ref, pools_ref,
                         convw_ref, convb_ref, clg_ref, clb_ref, glg_ref, glb_ref, wsp_ref,
                         bsf_ref, wout_ref, w1_ref, w3_ref, w2_ref,
                         o_ref, npool_ref, nconv_ref,
                         xa_ext, glu_ext, ycat, act, x1_buf, h2b, *, layer, n_t, n_tiles, final_norm):
    T = TOKEN_TILE
    s = pl.program_id(0)
    tile_a = jnp.minimum(s, n_tiles - 1)
    n_a = tile_a // n_t
    t_a = tile_a - n_a * n_t
    n_b = jnp.maximum(s - 1, 0) // n_t
    row = lambda ref: ref[layer:layer + 1, :]
    mod = lambda k, n: mod_ref[k, pl.ds(n, 1), :]

    @pl.when(t_a == 0)
    def _():
        xa_ext[0:POOL_HIST, :] = jnp.zeros((POOL_HIST, D_POOL), _F32)
        glu_ext[0:CONV_HIST, :] = jnp.zeros((CONV_HIST, D_CONV), _F32)
        glu_ext[CONV_HIST + T:CONV_HIST + T + SUBLANES, :] = jnp.zeros((SUBLANES, D_CONV), _F32)

    @pl.when(s == 0)
    def _():
        x1_buf[...] = jnp.zeros(x1_buf.shape, _F32)
        h2b[...] = jnp.zeros(h2b.shape, _BF16)

    def finish_mixer(x1):
        x1_buf[...] = x1
        h2 = _rms_norm(x1, row(ng2_ref)) * (1.0 + mod(4, n_a)) + mod(3, n_a)
        h2b[...] = h2.astype(_BF16)

    stages = {
        "a": _mixer_prompt_stages(
            x_ref, mod(0, n_a), mod(1, n_a), mod(2, n_a), row(ng1_ref), t_a, win_ref, poolw_ref,
            row(pools_ref), convw_ref, row(convb_ref), row(clg_ref), row(clb_ref), row(glg_ref),
            row(glb_ref), wsp_ref, bsf_ref, wout_ref, xa_ext, glu_ext, ycat, finish_mixer),
        "b": _ffn_prompt_stages(o_ref, x1_buf, h2b, mod(5, n_b), w1_ref, w3_ref, w2_ref,
                                fg_ref[...], act, final_norm),
    }
    for which in _STAGE_ORDER:
        next(stages[which])
    for gen in stages.values():
        assert next(gen, "done") == "done"

    @pl.when(jnp.logical_and(t_a == n_t - 1, s < n_tiles))
    def _():
        npool_ref[...] = xa_ext[pl.ds(POOL_HIST + T - POOL_BUF, POOL_BUF), :]
        nconv_ref[...] = glu_ext[pl.ds(CONV_HIST + T - CONV_BUF, CONV_BUF), :]


def _resident(shape, index):
    return pl.BlockSpec(shape, lambda *_: index, pipeline_mode=pl.Buffered(1))


def _param_specs(layer, depth):
    per_layer_rows = _resident((depth, D_MODEL), (0, 0))
    return dict(
        ng=per_layer_rows,
        fg=_resident((1, D_MODEL), (0, 0)),
        w_in=_resident((None, D_MODEL, D_IN), (layer, 0, 0)),
        pool_bd=_resident((None, D_POOL, D_POOL), (layer, 0, 0)),
        rows256=_resident((depth, D_POOL), (0, 0)),
        conv_w=_resident((None, CONV_WIDTH, D_CONV), (layer, 0, 0)),
        rows512=_resident((depth, D_GMLP), (0, 0)),
        ws_pairs=_resident((None, N_GMLP_HEADS // 2, CHUNK, 2 * CHUNK), (layer, 0, 0, 0)),
        bs_full=_resident((None, CHUNK, D_GMLP), (layer, 0, 0)),
        w_out=_resident((None, D_MODEL, D_MODEL), (layer, 0, 0)),
        w_ff13=_resident((None, D_MODEL, D_FF), (layer, 0, 0)),
        w_ff2=_resident((None, D_FF, D_MODEL), (layer, 0, 0)),
    )


def _layer_prompt_call(layer, x, mod, n_sample, P, final_norm):
    nb, seq, _ = x.shape
    depth = P["w_in"].shape[0]
    T = TOKEN_TILE
    n_t = seq // T
    n_tiles = nb * n_t
    sp = _param_specs(layer, depth)

    def x_map(s):
        tile = jnp.minimum(s, n_tiles - 1)
        return (tile // n_t, tile % n_t, 0)

    def o_map(s):
        tile = jnp.maximum(s - 1, 0)
        return (tile // n_t, tile % n_t, 0)

    def state_map(s):
        return (jnp.minimum(s, n_tiles - 1) // n_t, 0, 0)

    in_specs = [
        pl.BlockSpec((None, T, D_MODEL), x_map),
        pl.BlockSpec((None, N_MOD, nb, D_MODEL), lambda s: (layer, 0, n_sample // nb, 0),
                     pipeline_mode=pl.Buffered(1)),
        sp["ng"], sp["ng"], sp["fg"], sp["w_in"], sp["pool_bd"], sp["rows256"], sp["conv_w"],
        sp["rows256"], sp["rows256"], sp["rows256"], sp["rows512"], sp["rows512"], sp["ws_pairs"],
        sp["bs_full"], sp["w_out"], sp["w_ff13"], sp["w_ff13"], sp["w_ff2"],
    ]
    out_shape = (
        jax.ShapeDtypeStruct((nb, seq, D_MODEL), _F32),
        jax.ShapeDtypeStruct((nb, POOL_BUF, D_POOL), _F32),
        jax.ShapeDtypeStruct((nb, CONV_BUF, D_CONV), _F32),
    )
    out_specs = (
        pl.BlockSpec((None, T, D_MODEL), o_map),
        pl.BlockSpec((None, POOL_BUF, D_POOL), state_map),
        pl.BlockSpec((None, CONV_BUF, D_CONV), state_map),
    )
    return pl.pallas_call(
        functools.partial(_layer_prompt_kernel, layer=layer, n_t=n_t, n_tiles=n_tiles,
                          final_norm=final_norm),
        out_shape=out_shape,
        grid=(n_tiles + 1,),
        in_specs=in_specs,
        out_specs=out_specs,
        scratch_shapes=[
            pltpu.VMEM((T + POOL_HIST, D_POOL), _F32),
            pltpu.VMEM((T + CONV_HIST + SUBLANES, D_CONV), _F32),
            pltpu.VMEM((T, D_MODEL), _BF16),
            pltpu.VMEM((T, D_FF), _BF16),
            pltpu.VMEM((T, D_MODEL), _F32),
            pltpu.VMEM((T, D_MODEL), _BF16),
        ],
        compiler_params=pltpu.CompilerParams(
            dimension_semantics=("arbitrary",), vmem_limit_bytes=VMEM_LIMIT),
        name="layer_prompt",
    )(x, mod, P["norm1_g"], P["norm2_g"], P["final_g"], P["w_in"], P["pool_bd"], P["pool_scale"],
      P["conv_w"], P["conv_b"], P["conv_ln_g"], P["conv_ln_b"], P["gmlp_ln_g"], P["gmlp_ln_b"],
      P["ws_pairs"], P["bs_full"], P["w_out"], P["w_ff1"], P["w_ff3"], P["w_ff2"])


def _layer_sample_kernel(x_ref, mod_ref, ng1_ref, ng2_ref, fg_ref, win_ref, poolw_ref, pools_ref,
                         convw_ref, convb_ref, clg_ref, clb_ref, glg_ref, glb_ref, ws0_ref,
                         bs0_ref, wout_ref, w1_ref, w3_ref, w2_ref, spool_ref, sconv_ref,
                         o_ref, npool_ref, nconv_ref, vrow_ref, act, *, layer, final_norm):
    row = lambda ref: ref[layer:layer + 1, :]
    x = x_ref[...]
    h = _rms_norm(x, row(ng1_ref)) * (1.0 + mod_ref[1]) + mod_ref[0]
    hb = h.astype(_BF16)
    lo = _lo_lanes()

    xa = _dot(hb, win_ref[:, 0:D_POOL])
    rows = jax.lax.broadcasted_iota(jnp.int32, (1, POOL_BUF, LANES), 1)
    means = []
    for half in range(2):
        cols = slice(half * LANES, (half + 1) * LANES)
        w_a, w_b = POOL_WINDOWS[2 * half], POOL_WINDOWS[2 * half + 1]
        st = spool_ref[:, :, cols]
        first = jnp.where(lo, POOL_BUF + 1 - w_a, POOL_BUF + 1 - w_b)[None]
        s = xa[:, cols] + jnp.sum(jnp.where(rows >= first, st, 0.0), axis=1)
        means.append(s / jnp.where(lo, float(w_a), float(w_b)))
    ya = _pool_mix(means, xa, poolw_ref, row(pools_ref))
    npool_ref[:, 0:POOL_BUF - 1, :] = spool_ref[:, 1:POOL_BUF, :]
    npool_ref[:, POOL_BUF - 1:POOL_BUF, :] = xa[:, None, :]

    a = _dot(hb, win_ref[:, D_POOL:D_POOL + D_CONV])
    gt = _dot(hb, win_ref[:, D_POOL + D_CONV:D_POOL + 2 * D_CONV])
    glu = a * _sigmoid(gt)
    acc = (row(convb_ref) + convw_ref[CONV_BUF:CONV_WIDTH, :] * glu
           + jnp.sum(sconv_ref[...] * convw_ref[0:CONV_BUF, :][None], axis=1))
    yb = _silu(_layer_norm(acc, row(clg_ref), row(clb_ref)))
    nconv_ref[:, 0:CONV_BUF - 1, :] = sconv_ref[:, 1:CONV_BUF, :]
    nconv_ref[:, CONV_BUF - 1:CONV_BUF, :] = glu[:, None, :]

    off_u = D_POOL + 2 * D_CONV
    u = _gelu_tanh(_dot(hb, win_ref[:, off_u:off_u + D_GMLP]))
    v = _gelu_tanh(_dot(hb, win_ref[:, off_u + D_GMLP:off_u + 2 * D_GMLP]))
    v = _layer_norm(v, row(glg_ref), row(glb_ref))
    vrow_ref[...] = v
    yc = u * (row(ws0_ref) * v + row(bs0_ref))

    ycat = jnp.concatenate([ya, yb, yc], axis=-1).astype(_BF16)
    x1 = x + mod_ref[2] * _dot(ycat, wout_ref[...])
    o_ref[...] = _ffn_body(x1, mod_ref[3], mod_ref[4], mod_ref[5], row(ng2_ref), w1_ref, w3_ref,
                           w2_ref, fg_ref[...], act, final_norm)


def _layer_sample_call(layer, xs, mod, P, state_pool, state_conv, final_norm):
    n = xs.shape[0]
    depth = P["w_in"].shape[0]
    sp = _param_specs(layer, depth)
    in_specs = [
        _resident((n, D_MODEL), (0, 0)),
        _resident((None, N_MOD, n, D_MODEL), (layer, 0, 0, 0)),
        sp["ng"], sp["ng"], sp["fg"], sp["w_in"], sp["pool_bd"], sp["rows256"], sp["conv_w"],
        sp["rows256"], sp["rows256"], sp["rows256"], sp["rows512"], sp["rows512"], sp["rows512"],
        sp["rows512"], sp["w_out"], sp["w_ff13"], sp["w_ff13"], sp["w_ff2"],
        _resident((None, n, POOL_BUF, D_POOL), (layer, 0, 0, 0)),
        _resident((None, n, CONV_BUF, D_CONV), (layer, 0, 0, 0)),
    ]
    out_shape = (
        jax.ShapeDtypeStruct((n, D_MODEL), _F32),
        jax.ShapeDtypeStruct((n, POOL_BUF, D_POOL), _F32),
        jax.ShapeDtypeStruct((n, CONV_BUF, D_CONV), _F32),
        jax.ShapeDtypeStruct((n, D_GMLP), _F32),
    )
    whole = lambda shape: pl.BlockSpec(shape, lambda i: (0,) * len(shape))
    out_specs = (whole((n, D_MODEL)), whole((n, POOL_BUF, D_POOL)), whole((n, CONV_BUF, D_CONV)),
                 whole((n, D_GMLP)))
    return pl.pallas_call(
        functools.partial(_layer_sample_kernel, layer=layer, final_norm=final_norm),
        out_shape=out_shape,
        grid=(1,),
        in_specs=in_specs,
        out_specs=out_specs,
        scratch_shapes=[pltpu.VMEM((n, D_FF), _BF16)],
        compiler_params=pltpu.CompilerParams(
            dimension_semantics=("arbitrary",), vmem_limit_bytes=VMEM_LIMIT),
        name="layer_sample",
    )(xs, mod, P["norm1_g"], P["norm2_g"], P["final_g"], P["w_in"], P["pool_bd"], P["pool_scale"],
      P["conv_w"], P["conv_b"], P["conv_ln_g"], P["conv_ln_b"], P["gmlp_ln_g"], P["gmlp_ln_b"],
      P["ws0"], P["bs0"], P["w_out"], P["w_ff1"], P["w_ff3"], P["w_ff2"], state_pool, state_conv)


def _prep_params(norm1_g, norm2_g, w_in, pool_w, pool_scale, conv_w, conv_b, conv_ln_g, conv_ln_b,
                 gmlp_ln_g, gmlp_ln_b, gmlp_ws, gmlp_bs, w_out, w_ff1, w_ff3, w_ff2, final_g):
    depth = w_in.shape[0]
    n_grp = len(POOL_WINDOWS)
    eye = jnp.eye(n_grp, dtype=_F32)
    pool_bd = (eye[None, :, None, :, None] * pool_w[:, :, :, None, :]).reshape(depth, D_POOL, D_POOL)
    mask = jnp.tril(jnp.ones((CHUNK, CHUNK), dtype=bool))
    ws = jnp.where(mask, gmlp_ws, jnp.zeros_like(gmlp_ws))
    ws_pairs = ws.reshape(depth, N_GMLP_HEADS // 2, 2, CHUNK, CHUNK).transpose(0, 1, 3, 2, 4).reshape(
        depth, N_GMLP_HEADS // 2, CHUNK, 2 * CHUNK)
    bs_full = jnp.repeat(gmlp_bs.transpose(0, 2, 1), GMLP_HEAD, axis=2)
    ws0 = jnp.repeat(gmlp_ws[:, :, 0, 0], GMLP_HEAD, axis=1)
    bs0 = jnp.repeat(gmlp_bs[:, :, 0], GMLP_HEAD, axis=1)
    return dict(
        norm1_g=norm1_g, norm2_g=norm2_g, final_g=final_g.reshape(1, D_MODEL),
        w_in=w_in.astype(_BF16), pool_bd=pool_bd.astype(_BF16), pool_scale=pool_scale,
        conv_w=conv_w, conv_b=conv_b, conv_ln_g=conv_ln_g, conv_ln_b=conv_ln_b,
        gmlp_ln_g=gmlp_ln_g, gmlp_ln_b=gmlp_ln_b,
        ws_pairs=ws_pairs.astype(_BF16), bs_full=bs_full, ws0=ws0, bs0=bs0,
        w_out=w_out.astype(_BF16),
        w_ff1=w_ff1.astype(_BF16), w_ff3=w_ff3.astype(_BF16), w_ff2=w_ff2.astype(_BF16),
    )


def kernel(x_prompt, x_sample, c_prompt, c_sample, state_pool, state_conv, w_ada, b_ada, norm1_g, norm2_g, w_in, pool_w, pool_scale, conv_w, conv_b, conv_ln_g, conv_ln_b, gmlp_ln_g, gmlp_ln_b, gmlp_ws, gmlp_bs, w_out, w_ff1, w_ff3, w_ff2, final_g):
    depth = w_in.shape[0]
    nb = x_prompt.shape[0]
    ns = x_sample.shape[0]
    assert x_sample.shape[1] == 1 and x_prompt.shape[1] % TOKEN_TILE == 0 and ns % nb == 0

    c_all = jnp.concatenate([c_sample, c_prompt], axis=0)
    mod = _ada_call(c_all, w_ada, b_ada)
    P = _prep_params(norm1_g, norm2_g, w_in, pool_w, pool_scale, conv_w, conv_b, conv_ln_g,
                     conv_ln_b, gmlp_ln_g, gmlp_ln_b, gmlp_ws, gmlp_bs, w_out, w_ff1, w_ff3, w_ff2,
                     final_g)

    xp = x_prompt
    xs = x_sample.reshape(ns, D_MODEL)
    pool_p, conv_p, pool_s, conv_s, v_s = [], [], [], [], []
    for l in range(depth):
        last = l == depth - 1
        xp, npool_p, nconv_p = _layer_prompt_call(l, xp, mod, ns, P, last)
        xs, npool_s, nconv_s, nv_s = _layer_sample_call(l, xs, mod, P, state_pool, state_conv, last)
        pool_p.append(npool_p)
        conv_p.append(nconv_p)
        pool_s.append(npool_s)
        conv_s.append(nconv_s)
        v_s.append(nv_s.reshape(ns, 1, D_GMLP))
    return (xp, xs.reshape(ns, 1, D_MODEL), jnp.stack(pool_p), jnp.stack(conv_p),
            jnp.stack(pool_s), jnp.stack(conv_s), jnp.stack(v_s))
```

```python
import functools

import jax
import jax.numpy as jnp
from jax.experimental import pallas as pl
from jax.experimental.pallas import tpu as pltpu

D_MODEL = 1024
D_POOL = 256
POOL_WINDOWS = (2, 4, 8, 16)
POOL_GROUP = 64
POOL_BUF = 15
D_CONV = 256
CONV_WIDTH = 31
CONV_BUF = 30
D_GMLP = 512
GMLP_HEAD = 64
N_GMLP_HEADS = 8
CHUNK = 128
D_IN = D_POOL + 2 * D_CONV + 2 * D_GMLP
D_FF = 2816
N_MOD = 6
EPS = 1e-6

LANES = 128
SUBLANES = 8

TOKEN_TILE = 512
FF_CHUNK = 256
POOL_HIST = 16
CONV_HIST = 32
VMEM_LIMIT = 60 * 1024 * 1024

_BF16 = jnp.bfloat16
_F32 = jnp.float32


def _dot(a, b):
    return jnp.dot(a, b, preferred_element_type=_F32)


def _sigmoid(x):
    return 1.0 / (1.0 + jnp.exp(-x))


def _silu(x):
    return x * _sigmoid(x)


def _gelu_tanh(x):
    c = 0.7978845608028654
    return 0.5 * x * (1.0 + jnp.tanh(c * (x + 0.044715 * (x * x * x))))


def _rms_norm(x, g):
    ms = jnp.mean(x * x, axis=-1, keepdims=True)
    return (x * jax.lax.rsqrt(ms + EPS)) * g


def _layer_norm(x, g, b):
    mu = jnp.mean(x, axis=-1, keepdims=True)
    xc = x - mu
    var = jnp.mean(xc * xc, axis=-1, keepdims=True)
    return xc * jax.lax.rsqrt(var + EPS) * g + b


def _lo_lanes():
    return jax.lax.broadcasted_iota(jnp.int32, (1, LANES), 1) < POOL_GROUP


def _ada_kernel(c_ref, w_ref, b_ref, o_ref):
    l = pl.program_id(0)
    s = _silu(c_ref[...]).astype(_BF16)
    o_ref[...] = _dot(s, w_ref[...].astype(_BF16)) + b_ref[pl.ds(l, 1), :]


def _ada_call(c_all, w_ada, b_ada):
    depth = w_ada.shape[0]
    n = c_all.shape[0]
    return pl.pallas_call(
        _ada_kernel,
        out_shape=jax.ShapeDtypeStruct((depth, N_MOD, n, D_MODEL), _F32),
        grid=(depth, N_MOD),
        in_specs=[
            pl.BlockSpec((n, D_MODEL), lambda l, j: (0, 0)),
            pl.BlockSpec((None, D_MODEL, D_MODEL), lambda l, j: (l, 0, j)),
            pl.BlockSpec((depth, D_MODEL), lambda l, j: (0, j)),
        ],
        out_specs=pl.BlockSpec((None, None, n, D_MODEL), lambda l, j: (l, j, 0, 0)),
        compiler_params=pltpu.CompilerParams(
            dimension_semantics=("arbitrary", "arbitrary"), vmem_limit_bytes=VMEM_LIMIT),
        name="ada_mod",
    )(c_all, w_ada, b_ada)


def _pool_mix(means, xa, poolw_ref, pools):
    d = jnp.concatenate(means, axis=-1) - xa
    return _dot(d.astype(_BF16), poolw_ref[...]) * pools


def _ffn_body(x, sh, sc, g, ng, w1_ref, w3_ref, w2_ref, fg, act, final_norm):
    h = _rms_norm(x, ng) * (1.0 + sc) + sh
    hb = h.astype(_BF16)
    for c in range(D_FF // FF_CHUNK):
        cols = slice(c * FF_CHUNK, (c + 1) * FF_CHUNK)
        a1 = _dot(hb, w1_ref[:, cols])
        a3 = _dot(hb, w3_ref[:, cols])
        act[:, cols] = (_silu(a1) * a3).astype(_BF16)
    ff = _dot(act[...], w2_ref[...])
    out = x + g * ff
    if final_norm:
        out = _rms_norm(out, fg)
    return out


def _mixer_prompt_stages(x_ref, sh, sc, g, ng, t, win_ref, poolw_ref, pools, convw_ref, convb, clg, clb,
                         glg, glb, wsp_ref, bsf_ref, wout_ref, xa_ext, glu_ext, ycat, finish):
    T = TOKEN_TILE
    x = x_ref[...]
    h = _rms_norm(x, ng) * (1.0 + sc) + sh
    hb = h.astype(_BF16)
    lo = _lo_lanes()
    yield

    xa = _dot(hb, win_ref[:, 0:D_POOL])
    a = _dot(hb, win_ref[:, D_POOL:D_POOL + D_CONV])
    gt = _dot(hb, win_ref[:, D_POOL + D_CONV:D_POOL + 2 * D_CONV])
    yield
    xa_ext[POOL_HIST:POOL_HIST + T, :] = xa
    glu_ext[CONV_HIST:CONV_HIST + T, :] = a * _sigmoid(gt)
    pos1 =(t * T + 1 + jax.lax.broadcasted_iota(jnp.int32, (T, LANES), 0)).astype(_F32)
    means = []
    for half in range(2):
        w_a, w_b = POOL_WINDOWS[2 * half], POOL_WINDOWS[2 * half + 1]
        cur = xa_ext[:, half * LANES:(half + 1) * LANES]
        sums = {1: cur}
        w = 1
        while w < w_b:
            cur = cur + pltpu.roll(cur, w, axis=0)
            w *= 2
            sums[w] = cur
        s_a = sums[w_a][POOL_HIST:POOL_HIST + T, :]
        s_b = sums[w_b][POOL_HIST:POOL_HIST + T, :]
        cnt = jnp.minimum(pos1, jnp.where(lo, float(w_a), float(w_b)))
        means.append(jnp.where(lo, s_a, s_b) / cnt)
    ya = _pool_mix(means, xa, poolw_ref, pools)
    ycat[:, 0:D_POOL] = ya.astype(_BF16)
    yield

    off_u = D_POOL + 2 * D_CONV
    u = _gelu_tanh(_dot(hb, win_ref[:, off_u:off_u + D_GMLP]))
    v = _gelu_tanh(_dot(hb, win_ref[:, off_u + D_GMLP:off_u + 2 * D_GMLP]))
    v = _layer_norm(v, glg, glb)
    yield

    lead = CONV_HIST - CONV_BUF
    acc = convb
    for r in range(SUBLANES):
        part = None
        for q in range((lead + CONV_WIDTH - 1) // SUBLANES + 1):
            j = SUBLANES * q + r
            if lead <= j < lead + CONV_WIDTH:
                term = convw_ref[j - lead:j - lead + 1, :] * glu_ext[SUBLANES * q:SUBLANES * q + T + SUBLANES, :]
                part = term if part is None else part + term
        acc = acc + part[r:r + T, :]
    yb = _silu(_layer_norm(acc, clg, clb))
    ycat[:, D_POOL:D_POOL + D_CONV] = yb.astype(_BF16)
    yield

    zero = jnp.zeros((CHUNK, LANES), _F32)
    for cp in range(T // (2 * CHUNK)):
        r0 = cp * 2 * CHUNK
        for m in range(N_GMLP_HEADS // 2):
            l0 = m * LANES
            v0 = v[r0:r0 + CHUNK, l0:l0 + LANES]
            v1 = v[r0 + CHUNK:r0 + 2 * CHUNK, l0:l0 + LANES]
            top = jnp.concatenate([jnp.where(lo, v0, zero), jnp.where(lo, v1, zero)], axis=1)
            bot = jnp.concatenate([jnp.where(lo, zero, v0), jnp.where(lo, zero, v1)], axis=1)
            rhs = jnp.concatenate([top, bot], axis=0).astype(_BF16)
            z = _dot(wsp_ref[m], rhs)
            bias = bsf_ref[:, l0:l0 + LANES]
            y0 = u[r0:r0 + CHUNK, l0:l0 + LANES] * (z[:, 0:LANES] + bias)
            y1 = u[r0 + CHUNK:r0 + 2 * CHUNK, l0:l0 + LANES] * (z[:, LANES:2 * LANES] + bias)
            c0 = D_POOL + D_CONV + l0
            ycat[r0:r0 + CHUNK, c0:c0 + LANES] = y0.astype(_BF16)
            ycat[r0 + CHUNK:r0 + 2 * CHUNK, c0:c0 + LANES] = y1.astype(_BF16)
    yield

    y = _dot(ycat[...], wout_ref[...])
    finish(x + g * y)
    xa_ext[0:POOL_HIST, :] = xa_ext[T:T + POOL_HIST, :]
    glu_ext[0:CONV_HIST, :] = glu_ext[T:T + CONV_HIST, :]
    yield


def _ffn_prompt_stages(o_ref, x1_buf, h2b, g, w1_ref, w3_ref, w2_ref, fg, act, final_norm):
    o_ref[...] = x1_buf[...]
    yield
    for c in range(D_FF // FF_CHUNK):
        cols = slice(c * FF_CHUNK, (c + 1) * FF_CHUNK)
        a1 = _dot(h2b[...], w1_ref[:, cols])
        a3 = _dot(h2b[...], w3_ref[:, cols])
        act[:, cols] = (_silu(a1) * a3).astype(_BF16)
        yield
    out = o_ref[...] + g * _dot(act[...], w2_ref[...])
    if final_norm:
        out = _rms_norm(out, fg)
    o_ref[...] = out
    yield


_STAGE_ORDER = "ba" + "bba" + "bbaa" + "bbbbb" + "aa" + "bba" + "b"
_N_PROMPT_INPUTS = 20
_BIG_WEIGHTS = ("w_in", "w_out", "w_ff1", "w_ff3", "w_ff2")


def _layer_prompt_kernel(*refs, layer, n_t, n_tiles, final_norm, n_cast):
    n_in = _N_PROMPT_INPUTS + n_cast
    (x_ref, mod_ref, ng1_ref, ng2_ref, fg_ref, win_ref, poolw_ref, pools_ref, convw_ref, convb_ref,
     clg_ref, clb_ref, glg_ref, glb_ref, wsp_ref, bsf_ref, wout_ref, w1_ref, w3_ref,
     w2_ref) = refs[:_N_PROMPT_INPUTS]
    cast_src = refs[_N_PROMPT_INPUTS:n_in]
    o_ref, npool_ref, nconv_ref = refs[n_in:n_in + 3]
    cast_dst = refs[n_in + 3:n_in + 3 + n_cast]
    xa_ext, glu_ext, ycat, act, x1_buf, h2b = refs[n_in + 3 + n_cast:]
    T = TOKEN_TILE
    s = pl.program_id(0)
    tile_a = jnp.minimum(s, n_tiles - 1)
    n_a = tile_a // n_t
    t_a = tile_a - n_a * n_t
    n_b = jnp.maximum(s - 1, 0) // n_t
    row = lambda ref: ref[layer:layer + 1, :]
    mod = lambda k, n: mod_ref[k, pl.ds(n, 1), :]

    @pl.when(t_a == 0)
    def _():
        xa_ext[0:POOL_HIST, :] = jnp.zeros((POOL_HIST, D_POOL), _F32)
        glu_ext[0:CONV_HIST, :] = jnp.zeros((CONV_HIST, D_CONV), _F32)
        glu_ext[CONV_HIST + T:CONV_HIST + T + SUBLANES, :] = jnp.zeros((SUBLANES, D_CONV), _F32)

    @pl.when(s == 0)
    def _():
        x1_buf[...] = jnp.zeros(x1_buf.shape, _F32)
        h2b[...] = jnp.zeros(h2b.shape, _BF16)

    for src, dst in zip(cast_src, cast_dst):
        dst[...] = src[...].astype(_BF16)

    def finish_mixer(x1):
        x1_buf[...] = x1
        h2 = _rms_norm(x1, row(ng2_ref)) * (1.0 + mod(4, n_a)) + mod(3, n_a)
        h2b[...] = h2.astype(_BF16)

    stages = {
        "a": _mixer_prompt_stages(
            x_ref, mod(0, n_a), mod(1, n_a), mod(2, n_a), row(ng1_ref), t_a, win_ref, poolw_ref,
            row(pools_ref), convw_ref, row(convb_ref), row(clg_ref), row(clb_ref), row(glg_ref),
            row(glb_ref), wsp_ref, bsf_ref, wout_ref, xa_ext, glu_ext, ycat, finish_mixer),
        "b": _ffn_prompt_stages(o_ref, x1_buf, h2b, mod(5, n_b), w1_ref, w3_ref, w2_ref,
                                fg_ref[...], act, final_norm),
    }
    for which in _STAGE_ORDER:
        next(stages[which])
    for gen in stages.values():
        assert next(gen, "done") == "done"

    @pl.when(jnp.logical_and(t_a == n_t - 1, s < n_tiles))
    def _():
        npool_ref[...] = xa_ext[pl.ds(POOL_HIST + T - POOL_BUF, POOL_BUF), :]
        nconv_ref[...] = glu_ext[pl.ds(CONV_HIST + T - CONV_BUF, CONV_BUF), :]


def _resident(shape, index):
    return pl.BlockSpec(shape, lambda *_: index, pipeline_mode=pl.Buffered(1))


def _param_specs(layer, depth):
    per_layer_rows = _resident((depth, D_MODEL), (0, 0))
    return dict(
        ng=per_layer_rows,
        fg=_resident((1, D_MODEL), (0, 0)),
        w_in=_resident((D_MODEL, D_IN), (0, 0)),
        pool_bd=_resident((None, D_POOL, D_POOL), (layer, 0, 0)),
        rows256=_resident((depth, D_POOL), (0, 0)),
        conv_w=_resident((None, CONV_WIDTH, D_CONV), (layer, 0, 0)),
        rows512=_resident((depth, D_GMLP), (0, 0)),
        ws_pairs=_resident((None, N_GMLP_HEADS // 2, CHUNK, 2 * CHUNK), (layer, 0, 0, 0)),
        bs_full=_resident((None, CHUNK, D_GMLP), (layer, 0, 0)),
        w_out=_resident((D_MODEL, D_MODEL), (0, 0)),
        w_ff13=_resident((D_MODEL, D_FF), (0, 0)),
        w_ff2=_resident((D_FF, D_MODEL), (0, 0)),
    )


def _cast_rows(n_rows, n_steps):
    bf16_rows = 2 * SUBLANES
    for rows in range(bf16_rows, n_rows + 1, bf16_rows):
        if n_rows % rows == 0 and n_rows // rows <= n_steps:
            return rows
    raise ValueError((n_rows, n_steps))


def _layer_prompt_call(layer, x, mod, n_sample, P, W, next_f32, final_norm):
    nb, seq, _ = x.shape
    depth = P["norm1_g"].shape[0]
    T = TOKEN_TILE
    n_t = seq // T
    n_tiles = nb * n_t
    n_steps = n_tiles + 1
    sp = _param_specs(layer, depth)
    cast_in, cast_specs_in, cast_shapes, cast_specs_out = [], [], [], []
    if next_f32 is not None:
        for name in _BIG_WEIGHTS:
            w = next_f32[name]
            n_rows, n_cols = w.shape[1:]
            rows = _cast_rows(n_rows, n_steps)
            last = n_rows // rows - 1
            cast_in.append(w)
            cast_specs_in.append(pl.BlockSpec(
                (None, rows, n_cols), lambda s, last=last: (layer + 1, jnp.minimum(s, last), 0)))
            cast_shapes.append(jax.ShapeDtypeStruct((n_rows, n_cols), _BF16))
            cast_specs_out.append(pl.BlockSpec(
                (rows, n_cols), lambda s, last=last: (jnp.minimum(s, last), 0)))

    def x_map(s):
        tile = jnp.minimum(s, n_tiles - 1)
        return (tile // n_t, tile % n_t, 0)

    def o_map(s):
        tile = jnp.maximum(s - 1, 0)
        return (tile // n_t, tile % n_t, 0)

    def state_map(s):
        return (jnp.minimum(s, n_tiles - 1) // n_t, 0, 0)

    in_specs = [
        pl.BlockSpec((None, T, D_MODEL), x_map),
        pl.BlockSpec((None, N_MOD, nb, D_MODEL), lambda s: (layer, 0, n_sample // nb, 0),
                     pipeline_mode=pl.Buffered(1)),
        sp["ng"], sp["ng"], sp["fg"], sp["w_in"], sp["pool_bd"], sp["rows256"], sp["conv_w"],
        sp["rows256"], sp["rows256"], sp["rows256"], sp["rows512"], sp["rows512"], sp["ws_pairs"],
        sp["bs_full"], sp["w_out"], sp["w_ff13"], sp["w_ff13"], sp["w_ff2"],
    ]
    assert len(in_specs) == _N_PROMPT_INPUTS
    out_shape = [
        jax.ShapeDtypeStruct((nb, seq, D_MODEL), _F32),
        jax.ShapeDtypeStruct((nb, POOL_BUF, D_POOL), _F32),
        jax.ShapeDtypeStruct((nb, CONV_BUF, D_CONV), _F32),
    ]
    out_specs = [
        pl.BlockSpec((None, T, D_MODEL), o_map),
        pl.BlockSpec((None, POOL_BUF, D_POOL), state_map),
        pl.BlockSpec((None, CONV_BUF, D_CONV), state_map),
    ]
    outs = pl.pallas_call(
        functools.partial(_layer_prompt_kernel, layer=layer, n_t=n_t, n_tiles=n_tiles,
                          final_norm=final_norm, n_cast=len(cast_in)),
        out_shape=out_shape + cast_shapes,
        grid=(n_steps,),
        in_specs=in_specs + cast_specs_in,
        out_specs=out_specs + cast_specs_out,
        scratch_shapes=[
            pltpu.VMEM((T + POOL_HIST, D_POOL), _F32),
            pltpu.VMEM((T + CONV_HIST + SUBLANES, D_CONV), _F32),
            pltpu.VMEM((T, D_MODEL), _BF16),
            pltpu.VMEM((T, D_FF), _BF16),
            pltpu.VMEM((T, D_MODEL), _F32),
            pltpu.VMEM((T, D_MODEL), _BF16),
        ],
        compiler_params=pltpu.CompilerParams(
            dimension_semantics=("arbitrary",), vmem_limit_bytes=VMEM_LIMIT),
        name="layer_prompt",
    )(x, mod, P["norm1_g"], P["norm2_g"], P["final_g"], W["w_in"], P["pool_bd"], P["pool_scale"],
      P["conv_w"], P["conv_b"], P["conv_ln_g"], P["conv_ln_b"], P["gmlp_ln_g"], P["gmlp_ln_b"],
      P["ws_pairs"], P["bs_full"], W["w_out"], W["w_ff1"], W["w_ff3"], W["w_ff2"], *cast_in)
    next_w = dict(zip(_BIG_WEIGHTS, outs[3:])) if cast_in else None
    return outs[0], outs[1], outs[2], next_w


def _layer_sample_kernel(x_ref, mod_ref, ng1_ref, ng2_ref, fg_ref, win_ref, poolw_ref, pools_ref,
                         convw_ref, convb_ref, clg_ref, clb_ref, glg_ref, glb_ref, ws0_ref,
                         bs0_ref, wout_ref, w1_ref, w3_ref, w2_ref, spool_ref, sconv_ref,
                         o_ref, npool_ref, nconv_ref, vrow_ref, act, *, layer, final_norm):
    row = lambda ref: ref[layer:layer + 1, :]
    x = x_ref[...]
    h = _rms_norm(x, row(ng1_ref)) * (1.0 + mod_ref[1]) + mod_ref[0]
    hb = h.astype(_BF16)
    lo = _lo_lanes()

    xa = _dot(hb, win_ref[:, 0:D_POOL])
    means = []
    for half in range(2):
        cols = slice(half * LANES, (half + 1) * LANES)
        w_a, w_b = POOL_WINDOWS[2 * half], POOL_WINDOWS[2 * half + 1]
        s = xa[:, cols]
        for j in range(1, w_a):
            s = s + spool_ref[POOL_BUF - j, :, cols]
        s_a = s
        for j in range(w_a, w_b):
            s = s + spool_ref[POOL_BUF - j, :, cols]
        means.append(jnp.where(lo, s_a, s) / jnp.where(lo, float(w_a), float(w_b)))
    ya = _pool_mix(means, xa, poolw_ref, row(pools_ref))
    npool_ref[0:POOL_BUF - 1] = spool_ref[1:POOL_BUF]
    npool_ref[POOL_BUF - 1] = xa

    a = _dot(hb, win_ref[:, D_POOL:D_POOL + D_CONV])
    gt = _dot(hb, win_ref[:, D_POOL + D_CONV:D_POOL + 2 * D_CONV])
    glu = a * _sigmoid(gt)
    acc = row(convb_ref) + convw_ref[CONV_BUF:CONV_WIDTH, :] * glu
    for r in range(CONV_BUF):
        acc = acc + convw_ref[r:r + 1, :] * sconv_ref[r]
    yb = _silu(_layer_norm(acc, row(clg_ref), row(clb_ref)))
    nconv_ref[0:CONV_BUF - 1] = sconv_ref[1:CONV_BUF]
    nconv_ref[CONV_BUF - 1] = glu

    off_u = D_POOL + 2 * D_CONV
    u = _gelu_tanh(_dot(hb, win_ref[:, off_u:off_u + D_GMLP]))
    v = _gelu_tanh(_dot(hb, win_ref[:, off_u + D_GMLP:off_u + 2 * D_GMLP]))
    v = _layer_norm(v, row(glg_ref), row(glb_ref))
    vrow_ref[...] = v
    yc = u * (row(ws0_ref) * v + row(bs0_ref))

    ycat = jnp.concatenate([ya, yb, yc], axis=-1).astype(_BF16)
    x1 = x + mod_ref[2] * _dot(ycat, wout_ref[...])
    o_ref[...] = _ffn_body(x1, mod_ref[3], mod_ref[4], mod_ref[5], row(ng2_ref), w1_ref, w3_ref,
                           w2_ref, fg_ref[...], act, final_norm)


def _layer_sample_call(layer, xs, mod, P, W, state_pool, state_conv, final_norm):
    n = xs.shape[0]
    depth = P["norm1_g"].shape[0]
    sp = _param_specs(layer, depth)
    in_specs = [
        _resident((n, D_MODEL), (0, 0)),
        _resident((None, N_MOD, n, D_MODEL), (layer, 0, 0, 0)),
        sp["ng"], sp["ng"], sp["fg"], sp["w_in"], sp["pool_bd"], sp["rows256"], sp["conv_w"],
        sp["rows256"], sp["rows256"], sp["rows256"], sp["rows512"], sp["rows512"], sp["rows512"],
        sp["rows512"], sp["w_out"], sp["w_ff13"], sp["w_ff13"], sp["w_ff2"],
        _resident((None, POOL_BUF, n, D_POOL), (layer, 0, 0, 0)),
        _resident((None, CONV_BUF, n, D_CONV), (layer, 0, 0, 0)),
    ]
    out_shape = (
        jax.ShapeDtypeStruct((n, D_MODEL), _F32),
        jax.ShapeDtypeStruct((POOL_BUF, n, D_POOL), _F32),
        jax.ShapeDtypeStruct((CONV_BUF, n, D_CONV), _F32),
        jax.ShapeDtypeStruct((n, D_GMLP), _F32),
    )
    whole = lambda shape: pl.BlockSpec(shape, lambda i: (0,) * len(shape))
    out_specs = (whole((n, D_MODEL)), whole((POOL_BUF, n, D_POOL)), whole((CONV_BUF, n, D_CONV)),
                 whole((n, D_GMLP)))
    return pl.pallas_call(
        functools.partial(_layer_sample_kernel, layer=layer, final_norm=final_norm),
        out_shape=out_shape,
        grid=(1,),
        in_specs=in_specs,
        out_specs=out_specs,
        scratch_shapes=[pltpu.VMEM((n, D_FF), _BF16)],
        compiler_params=pltpu.CompilerParams(
            dimension_semantics=("arbitrary",), vmem_limit_bytes=VMEM_LIMIT),
        name="layer_sample",
    )(xs, mod, P["norm1_g"], P["norm2_g"], P["final_g"], W["w_in"], P["pool_bd"], P["pool_scale"],
      P["conv_w"], P["conv_b"], P["conv_ln_g"], P["conv_ln_b"], P["gmlp_ln_g"], P["gmlp_ln_b"],
      P["ws0"], P["bs0"], W["w_out"], W["w_ff1"], W["w_ff3"], W["w_ff2"], state_pool, state_conv)


def _prep_params(norm1_g, norm2_g, pool_w, pool_scale, conv_w, conv_b, conv_ln_g, conv_ln_b,
                 gmlp_ln_g, gmlp_ln_b, gmlp_ws, gmlp_bs, final_g):
    depth = pool_w.shape[0]
    n_grp = len(POOL_WINDOWS)
    eye = jnp.eye(n_grp, dtype=_F32)
    pool_bd = (eye[None, :, None, :, None] * pool_w[:, :, :, None, :]).reshape(depth, D_POOL, D_POOL)
    mask = jnp.tril(jnp.ones((CHUNK, CHUNK), dtype=bool))
    ws = jnp.where(mask, gmlp_ws, jnp.zeros_like(gmlp_ws))
    ws_pairs = ws.reshape(depth, N_GMLP_HEADS // 2, 2, CHUNK, CHUNK).transpose(0, 1, 3, 2, 4).reshape(
        depth, N_GMLP_HEADS // 2, CHUNK, 2 * CHUNK)
    bs_full = jnp.repeat(gmlp_bs.transpose(0, 2, 1), GMLP_HEAD, axis=2)
    ws0 = jnp.repeat(gmlp_ws[:, :, 0, 0], GMLP_HEAD, axis=1)
    bs0 = jnp.repeat(gmlp_bs[:, :, 0], GMLP_HEAD, axis=1)
    return dict(
        norm1_g=norm1_g, norm2_g=norm2_g, final_g=final_g.reshape(1, D_MODEL),
        pool_bd=pool_bd.astype(_BF16), pool_scale=pool_scale,
        conv_w=conv_w, conv_b=conv_b, conv_ln_g=conv_ln_g, conv_ln_b=conv_ln_b,
        gmlp_ln_g=gmlp_ln_g, gmlp_ln_b=gmlp_ln_b,
        ws_pairs=ws_pairs.astype(_BF16), bs_full=bs_full, ws0=ws0, bs0=bs0,
    )


def kernel(x_prompt, x_sample, c_prompt, c_sample, state_pool, state_conv, w_ada, b_ada, norm1_g, norm2_g, w_in, pool_w, pool_scale, conv_w, conv_b, conv_ln_g, conv_ln_b, gmlp_ln_g, gmlp_ln_b, gmlp_ws, gmlp_bs, w_out, w_ff1, w_ff3, w_ff2, final_g):
    depth = w_in.shape[0]
    nb = x_prompt.shape[0]
    ns = x_sample.shape[0]
    assert x_sample.shape[1] == 1 and x_prompt.shape[1] % TOKEN_TILE == 0 and ns % nb == 0

    c_all = jnp.concatenate([c_sample, c_prompt], axis=0)
    mod = _ada_call(c_all, w_ada, b_ada)
    P = _prep_params(norm1_g, norm2_g, pool_w, pool_scale, conv_w, conv_b, conv_ln_g, conv_ln_b,
                     gmlp_ln_g, gmlp_ln_b, gmlp_ws, gmlp_bs, final_g)
    big_f32 = dict(w_in=w_in, w_out=w_out, w_ff1=w_ff1, w_ff3=w_ff3, w_ff2=w_ff2)
    W = {name: w[0].astype(_BF16) for name, w in big_f32.items()}

    xp = x_prompt
    xs = x_sample.reshape(ns, D_MODEL)
    spool_t = state_pool.transpose(0, 2, 1, 3)
    sconv_t = state_conv.transpose(0, 2, 1, 3)
    pool_p, conv_p, pool_s, conv_s, v_s = [], [], [], [], []
    for l in range(depth):
        last = l == depth - 1
        xp, npool_p, nconv_p, next_w = _layer_prompt_call(
            l, xp, mod, ns, P, W, None if last else big_f32, last)
        xs, npool_s, nconv_s, nv_s = _layer_sample_call(l, xs, mod, P, W, spool_t, sconv_t, last)
        W = next_w
        pool_p.append(npool_p)
        conv_p.append(nconv_p)
        pool_s.append(npool_s)
        conv_s.append(nconv_s)
        v_s.append(nv_s.reshape(ns, 1, D_GMLP))
    return (xp, xs.reshape(ns, 1, D_MODEL), jnp.stack(pool_p), jnp.stack(conv_p),
            jnp.stack(pool_s).transpose(0, 2, 1, 3), jnp.stack(conv_s).transpose(0, 2, 1, 3),
            jnp.stack(v_s))
```

```python
import functools

import jax
import jax.numpy as jnp
from jax.experimental import pallas as pl
from jax.experimental.pallas import tpu as pltpu

D_MODEL = 1024
D_POOL = 256
POOL_WINDOWS = (2, 4, 8, 16)
POOL_GROUP = 64
POOL_BUF = 15
D_CONV = 256
CONV_WIDTH = 31
CONV_BUF = 30
D_GMLP = 512
GMLP_HEAD = 64
N_GMLP_HEADS = 8
CHUNK = 128
D_IN = D_POOL + 2 * D_CONV + 2 * D_GMLP
D_FF = 2816
N_MOD = 6
EPS = 1e-6

LANES = 128
SUBLANES = 8

TOKEN_TILE = 512
FF_CHUNK = 256
POOL_HIST = 16
CONV_HIST = 32
VMEM_LIMIT = 60 * 1024 * 1024

_BF16 = jnp.bfloat16
_F32 = jnp.float32


def _dot(a, b):
    return jnp.dot(a, b, preferred_element_type=_F32)


_LOG2E = 1.4426950408889634


def _sigmoid(x):
    return 1.0 / (1.0 + jnp.exp2(x * -_LOG2E))


def _silu(x):
    return x * _sigmoid(x)


def _gelu_tanh(x):
    k1 = -2.0 * 0.7978845608028654 * _LOG2E
    k3 = k1 * 0.044715
    return x / (1.0 + jnp.exp2(x * (k1 + k3 * (x * x))))


def _rms_norm(x, g):
    ms = jnp.mean(x * x, axis=-1, keepdims=True)
    return (x * jax.lax.rsqrt(ms + EPS)) * g


def _mod_norm(x, g, sc, sh):
    ms = jnp.mean(x * x, axis=-1, keepdims=True)
    return (x * jax.lax.rsqrt(ms + EPS)) * (g * (1.0 + sc)) + sh


def _layer_norm(x, g, b):
    mu = jnp.mean(x, axis=-1, keepdims=True)
    xc = x - mu
    var = jnp.mean(xc * xc, axis=-1, keepdims=True)
    return xc * jax.lax.rsqrt(var + EPS) * g + b


def _lo_lanes():
    return jax.lax.broadcasted_iota(jnp.int32, (1, LANES), 1) < POOL_GROUP


def _ada_kernel(c_ref, w_ref, b_ref, o_ref):
    l = pl.program_id(0)
    s = _silu(c_ref[...]).astype(_BF16)
    o_ref[...] = _dot(s, w_ref[...].astype(_BF16)) + b_ref[pl.ds(l, 1), :]


def _ada_call(c_all, w_ada, b_ada):
    depth = w_ada.shape[0]
    n = c_all.shape[0]
    return pl.pallas_call(
        _ada_kernel,
        out_shape=jax.ShapeDtypeStruct((depth, N_MOD, n, D_MODEL), _F32),
        grid=(depth, N_MOD),
        in_specs=[
            pl.BlockSpec((n, D_MODEL), lambda l, j: (0, 0)),
            pl.BlockSpec((None, D_MODEL, D_MODEL), lambda l, j: (l, 0, j)),
            pl.BlockSpec((depth, D_MODEL), lambda l, j: (0, j)),
        ],
        out_specs=pl.BlockSpec((None, None, n, D_MODEL), lambda l, j: (l, j, 0, 0)),
        compiler_params=pltpu.CompilerParams(
            dimension_semantics=("arbitrary", "arbitrary"), vmem_limit_bytes=VMEM_LIMIT),
        name="ada_mod",
    )(c_all, w_ada, b_ada)


def _pool_mix(means, xa, poolw_ref, pools):
    d = jnp.concatenate(means, axis=-1) - xa
    return _dot(d.astype(_BF16), poolw_ref[...]) * pools


def _ffn_body(x, sh, sc, g, ng, w1_ref, w3_ref, w2_ref, fg, act, final_norm):
    h = _mod_norm(x, ng, sc, sh)
    hb = h.astype(_BF16)
    for c in range(D_FF // FF_CHUNK):
        cols = slice(c * FF_CHUNK, (c + 1) * FF_CHUNK)
        a1 = _dot(hb, w1_ref[:, cols])
        a3 = _dot(hb, w3_ref[:, cols])
        act[:, cols] = (_silu(a1) * a3).astype(_BF16)
    ff = _dot(act[...], w2_ref[...])
    out = x + g * ff
    if final_norm:
        out = _rms_norm(out, fg)
    return out


def _mixer_prompt_stages(x_ref, sh, sc, g, ng, t, win_ref, poolw_ref, pools, convw_ref, convb, clg, clb,
                         glg, glb, wsp_ref, bsf_ref, wout_ref, xa_ext, glu_ext, ycat, finish):
    T = TOKEN_TILE
    x = x_ref[...]
    h = _mod_norm(x, ng, sc, sh)
    hb = h.astype(_BF16)
    lo = _lo_lanes()
    yield

    xa = _dot(hb, win_ref[:, 0:D_POOL])
    a = _dot(hb, win_ref[:, D_POOL:D_POOL + D_CONV])
    gt = _dot(hb, win_ref[:, D_POOL + D_CONV:D_POOL + 2 * D_CONV])
    yield
    xa_ext[POOL_HIST:POOL_HIST + T, :] = xa
    glu_ext[CONV_HIST:CONV_HIST + T, :] = a * _sigmoid(gt)
    pos1 =(t * T + 1 + jax.lax.broadcasted_iota(jnp.int32, (T, LANES), 0)).astype(_F32)
    means = []
    for half in range(2):
        w_a, w_b = POOL_WINDOWS[2 * half], POOL_WINDOWS[2 * half + 1]
        cur = xa_ext[:, half * LANES:(half + 1) * LANES]
        sums = {1: cur}
        w = 1
        while w < w_b:
            cur = cur + pltpu.roll(cur, w, axis=0)
            w *= 2
            sums[w] = cur
        s_a = sums[w_a][POOL_HIST:POOL_HIST + T, :]
        s_b = sums[w_b][POOL_HIST:POOL_HIST + T, :]
        cnt = jnp.minimum(pos1, jnp.where(lo, float(w_a), float(w_b)))
        means.append(jnp.where(lo, s_a, s_b) / cnt)
    ya = _pool_mix(means, xa, poolw_ref, pools)
    ycat[:, 0:D_POOL] = ya.astype(_BF16)
    yield

    off_u = D_POOL + 2 * D_CONV
    u = _gelu_tanh(_dot(hb, win_ref[:, off_u:off_u + D_GMLP]))
    v = _gelu_tanh(_dot(hb, win_ref[:, off_u + D_GMLP:off_u + 2 * D_GMLP]))
    v = _layer_norm(v, glg, glb)
    yield

    lead = CONV_HIST - CONV_BUF
    acc = convb
    for r in range(SUBLANES):
        part = None
        for q in range((lead + CONV_WIDTH - 1) // SUBLANES + 1):
            j = SUBLANES * q + r
            if lead <= j < lead + CONV_WIDTH:
                term = convw_ref[j - lead:j - lead + 1, :] * glu_ext[SUBLANES * q:SUBLANES * q + T + SUBLANES, :]
                part = term if part is None else part + term
        acc = acc + part[r:r + T, :]
    yb = _silu(_layer_norm(acc, clg, clb))
    ycat[:, D_POOL:D_POOL + D_CONV] = yb.astype(_BF16)
    yield

    zero = jnp.zeros((CHUNK, LANES), _F32)
    for cp in range(T // (2 * CHUNK)):
        r0 = cp * 2 * CHUNK
        for m in range(N_GMLP_HEADS // 2):
            l0 = m * LANES
            v0 = v[r0:r0 + CHUNK, l0:l0 + LANES]
            v1 = v[r0 + CHUNK:r0 + 2 * CHUNK, l0:l0 + LANES]
            top = jnp.concatenate([jnp.where(lo, v0, zero), jnp.where(lo, v1, zero)], axis=1)
            bot = jnp.concatenate([jnp.where(lo, zero, v0), jnp.where(lo, zero, v1)], axis=1)
            rhs = jnp.concatenate([top, bot], axis=0).astype(_BF16)
            z = _dot(wsp_ref[m], rhs)
            bias = bsf_ref[:, l0:l0 + LANES]
            y0 = u[r0:r0 + CHUNK, l0:l0 + LANES] * (z[:, 0:LANES] + bias)
            y1 = u[r0 + CHUNK:r0 + 2 * CHUNK, l0:l0 + LANES] * (z[:, LANES:2 * LANES] + bias)
            c0 = D_POOL + D_CONV + l0
            ycat[r0:r0 + CHUNK, c0:c0 + LANES] = y0.astype(_BF16)
            ycat[r0 + CHUNK:r0 + 2 * CHUNK, c0:c0 + LANES] = y1.astype(_BF16)
    yield

    y = _dot(ycat[...], wout_ref[...])
    finish(x + g * y)
    xa_ext[0:POOL_HIST, :] = xa_ext[T:T + POOL_HIST, :]
    glu_ext[0:CONV_HIST, :] = glu_ext[T:T + CONV_HIST, :]
    yield


def _ffn_prompt_stages(o_ref, x1_buf, h2b, g, w1_ref, w3_ref, w2_ref, fg, act, final_norm):
    o_ref[...] = x1_buf[...]
    yield
    for c in range(D_FF // FF_CHUNK):
        cols = slice(c * FF_CHUNK, (c + 1) * FF_CHUNK)
        a1 = _dot(h2b[...], w1_ref[:, cols])
        a3 = _dot(h2b[...], w3_ref[:, cols])
        act[:, cols] = (_silu(a1) * a3).astype(_BF16)
        yield
    out = o_ref[...] + g * _dot(act[...], w2_ref[...])
    if final_norm:
        out = _rms_norm(out, fg)
    o_ref[...] = out
    yield


_STAGE_ORDER = "ba" + "bba" + "bbaa" + "bbbbb" + "aa" + "bba" + "b"
_N_PROMPT_INPUTS = 20
_BIG_WEIGHTS = ("w_in", "w_out", "w_ff1", "w_ff3", "w_ff2")


def _layer_prompt_kernel(*refs, layer, n_t, n_tiles, final_norm, n_cast):
    n_in = _N_PROMPT_INPUTS + n_cast
    (x_ref, mod_ref, ng1_ref, ng2_ref, fg_ref, win_ref, poolw_ref, pools_ref, convw_ref, convb_ref,
     clg_ref, clb_ref, glg_ref, glb_ref, wsp_ref, bsf_ref, wout_ref, w1_ref, w3_ref,
     w2_ref) = refs[:_N_PROMPT_INPUTS]
    cast_src = refs[_N_PROMPT_INPUTS:n_in]
    o_ref, npool_ref, nconv_ref = refs[n_in:n_in + 3]
    cast_dst = refs[n_in + 3:n_in + 3 + n_cast]
    xa_ext, glu_ext, ycat, act, x1_buf, h2b = refs[n_in + 3 + n_cast:]
    T = TOKEN_TILE
    s = pl.program_id(0)
    tile_a = jnp.minimum(s, n_tiles - 1)
    n_a = tile_a // n_t
    t_a = tile_a - n_a * n_t
    n_b = jnp.maximum(s - 1, 0) // n_t
    row = lambda ref: ref[layer:layer + 1, :]
    mod = lambda k, n: mod_ref[k, pl.ds(n, 1), :]

    @pl.when(t_a == 0)
    def _():
        xa_ext[0:POOL_HIST, :] = jnp.zeros((POOL_HIST, D_POOL), _F32)
        glu_ext[0:CONV_HIST, :] = jnp.zeros((CONV_HIST, D_CONV), _F32)
        glu_ext[CONV_HIST + T:CONV_HIST + T + SUBLANES, :] = jnp.zeros((SUBLANES, D_CONV), _F32)

    @pl.when(s == 0)
    def _():
        x1_buf[...] = jnp.zeros(x1_buf.shape, _F32)
        h2b[...] = jnp.zeros(h2b.shape, _BF16)

    for src, dst in zip(cast_src, cast_dst):
        dst[...] = src[...].astype(_BF16)

    def finish_mixer(x1):
        x1_buf[...] = x1
        h2 = _mod_norm(x1, row(ng2_ref), mod(4, n_a), mod(3, n_a))
        h2b[...] = h2.astype(_BF16)

    stages = {
        "a": _mixer_prompt_stages(
            x_ref, mod(0, n_a), mod(1, n_a), mod(2, n_a), row(ng1_ref), t_a, win_ref, poolw_ref,
            row(pools_ref), convw_ref, row(convb_ref), row(clg_ref), row(clb_ref), row(glg_ref),
            row(glb_ref), wsp_ref, bsf_ref, wout_ref, xa_ext, glu_ext, ycat, finish_mixer),
        "b": _ffn_prompt_stages(o_ref, x1_buf, h2b, mod(5, n_b), w1_ref, w3_ref, w2_ref,
                                fg_ref[...], act, final_norm),
    }
    for which in _STAGE_ORDER:
        next(stages[which])
    for gen in stages.values():
        assert next(gen, "done") == "done"

    @pl.when(jnp.logical_and(t_a == n_t - 1, s < n_tiles))
    def _():
        npool_ref[...] = xa_ext[pl.ds(POOL_HIST + T - POOL_BUF, POOL_BUF), :]
        nconv_ref[...] = glu_ext[pl.ds(CONV_HIST + T - CONV_BUF, CONV_BUF), :]


def _resident(shape, index):
    return pl.BlockSpec(shape, lambda *_: index, pipeline_mode=pl.Buffered(1))


def _param_specs(layer, depth):
    per_layer_rows = _resident((depth, D_MODEL), (0, 0))
    return dict(
        ng=per_layer_rows,
        fg=_resident((1, D_MODEL), (0, 0)),
        w_in=_resident((D_MODEL, D_IN), (0, 0)),
        pool_bd=_resident((None, D_POOL, D_POOL), (layer, 0, 0)),
        rows256=_resident((depth, D_POOL), (0, 0)),
        conv_w=_resident((None, CONV_WIDTH, D_CONV), (layer, 0, 0)),
        rows512=_resident((depth, D_GMLP), (0, 0)),
        ws_pairs=_resident((None, N_GMLP_HEADS // 2, CHUNK, 2 * CHUNK), (layer, 0, 0, 0)),
        bs_full=_resident((None, CHUNK, D_GMLP), (layer, 0, 0)),
        w_out=_resident((D_MODEL, D_MODEL), (0, 0)),
        w_ff13=_resident((D_MODEL, D_FF), (0, 0)),
        w_ff2=_resident((D_FF, D_MODEL), (0, 0)),
    )


def _cast_rows(n_rows, n_steps):
    bf16_rows = 2 * SUBLANES
    for rows in range(bf16_rows, n_rows + 1, bf16_rows):
        if n_rows % rows == 0 and n_rows // rows <= n_steps:
            return rows
    raise ValueError((n_rows, n_steps))


def _layer_prompt_call(layer, x, mod, n_sample, P, W, next_f32, final_norm):
    nb, seq, _ = x.shape
    depth = P["norm1_g"].shape[0]
    T = TOKEN_TILE
    n_t = seq // T
    n_tiles = nb * n_t
    n_steps = n_tiles + 1
    sp = _param_specs(layer, depth)
    cast_in, cast_specs_in, cast_shapes, cast_specs_out = [], [], [], []
    if next_f32 is not None:
        for name in _BIG_WEIGHTS:
            w = next_f32[name]
            n_rows, n_cols = w.shape[1:]
            rows = _cast_rows(n_rows, n_steps)
            last = n_rows // rows - 1
            cast_in.append(w)
            cast_specs_in.append(pl.BlockSpec(
                (None, rows, n_cols), lambda s, last=last: (layer + 1, jnp.minimum(s, last), 0)))
            cast_shapes.append(jax.ShapeDtypeStruct((n_rows, n_cols), _BF16))
            cast_specs_out.append(pl.BlockSpec(
                (rows, n_cols), lambda s, last=last: (jnp.minimum(s, last), 0)))

    def x_map(s):
        tile = jnp.minimum(s, n_tiles - 1)
        return (tile // n_t, tile % n_t, 0)

    def o_map(s):
        tile = jnp.maximum(s - 1, 0)
        return (tile // n_t, tile % n_t, 0)

    def state_map(s):
        return (jnp.minimum(s, n_tiles - 1) // n_t, 0, 0)

    in_specs = [
        pl.BlockSpec((None, T, D_MODEL), x_map),
        pl.BlockSpec((None, N_MOD, nb, D_MODEL), lambda s: (layer, 0, n_sample // nb, 0),
                     pipeline_mode=pl.Buffered(1)),
        sp["ng"], sp["ng"], sp["fg"], sp["w_in"], sp["pool_bd"], sp["rows256"], sp["conv_w"],
        sp["rows256"], sp["rows256"], sp["rows256"], sp["rows512"], sp["rows512"], sp["ws_pairs"],
        sp["bs_full"], sp["w_out"], sp["w_ff13"], sp["w_ff13"], sp["w_ff2"],
    ]
    assert len(in_specs) == _N_PROMPT_INPUTS
    out_shape = [
        jax.ShapeDtypeStruct((nb, seq, D_MODEL), _F32),
        jax.ShapeDtypeStruct((nb, POOL_BUF, D_POOL), _F32),
        jax.ShapeDtypeStruct((nb, CONV_BUF, D_CONV), _F32),
    ]
    out_specs = [
        pl.BlockSpec((None, T, D_MODEL), o_map),
        pl.BlockSpec((None, POOL_BUF, D_POOL), state_map),
        pl.BlockSpec((None, CONV_BUF, D_CONV), state_map),
    ]
    outs = pl.pallas_call(
        functools.partial(_layer_prompt_kernel, layer=layer, n_t=n_t, n_tiles=n_tiles,
                          final_norm=final_norm, n_cast=len(cast_in)),
        out_shape=out_shape + cast_shapes,
        grid=(n_steps,),
        in_specs=in_specs + cast_specs_in,
        out_specs=out_specs + cast_specs_out,
        scratch_shapes=[
            pltpu.VMEM((T + POOL_HIST, D_POOL), _F32),
            pltpu.VMEM((T + CONV_HIST + SUBLANES, D_CONV), _F32),
            pltpu.VMEM((T, D_MODEL), _BF16),
            pltpu.VMEM((T, D_FF), _BF16),
            pltpu.VMEM((T, D_MODEL), _F32),
            pltpu.VMEM((T, D_MODEL), _BF16),
        ],
        compiler_params=pltpu.CompilerParams(
            dimension_semantics=("arbitrary",), vmem_limit_bytes=VMEM_LIMIT),
        name="layer_prompt",
    )(x, mod, P["norm1_g"], P["norm2_g"], P["final_g"], W["w_in"], P["pool_bd"], P["pool_scale"],
      P["conv_w"], P["conv_b"], P["conv_ln_g"], P["conv_ln_b"], P["gmlp_ln_g"], P["gmlp_ln_b"],
      P["ws_pairs"], P["bs_full"], W["w_out"], W["w_ff1"], W["w_ff3"], W["w_ff2"], *cast_in)
    next_w = dict(zip(_BIG_WEIGHTS, outs[3:])) if cast_in else None
    return outs[0], outs[1], outs[2], next_w


def _layer_sample_kernel(x_ref, mod_ref, ng1_ref, ng2_ref, fg_ref, win_ref, poolw_ref, pools_ref,
                         convw_ref, convb_ref, clg_ref, clb_ref, glg_ref, glb_ref, ws0_ref,
                         bs0_ref, wout_ref, w1_ref, w3_ref, w2_ref, spool_ref, sconv_ref,
                         o_ref, npool_ref, nconv_ref, vrow_ref, act, *, layer, final_norm):
    row = lambda ref: ref[layer:layer + 1, :]
    x = x_ref[...]
    h = _mod_norm(x, row(ng1_ref), mod_ref[1], mod_ref[0])
    hb = h.astype(_BF16)
    lo = _lo_lanes()

    xa = _dot(hb, win_ref[:, 0:D_POOL])
    means = []
    for half in range(2):
        cols = slice(half * LANES, (half + 1) * LANES)
        w_a, w_b = POOL_WINDOWS[2 * half], POOL_WINDOWS[2 * half + 1]
        s = xa[:, cols]
        for j in range(1, w_a):
            s = s + spool_ref[POOL_BUF - j, :, cols]
        s_a = s
        for j in range(w_a, w_b):
            s = s + spool_ref[POOL_BUF - j, :, cols]
        means.append(jnp.where(lo, s_a, s) / jnp.where(lo, float(w_a), float(w_b)))
    ya = _pool_mix(means, xa, poolw_ref, row(pools_ref))
    npool_ref[0:POOL_BUF - 1] = spool_ref[1:POOL_BUF]
    npool_ref[POOL_BUF - 1] = xa

    a = _dot(hb, win_ref[:, D_POOL:D_POOL + D_CONV])
    gt = _dot(hb, win_ref[:, D_POOL + D_CONV:D_POOL + 2 * D_CONV])
    glu = a * _sigmoid(gt)
    acc = row(convb_ref) + convw_ref[CONV_BUF:CONV_WIDTH, :] * glu
    for r in range(CONV_BUF):
        acc = acc + convw_ref[r:r + 1, :] * sconv_ref[r]
    yb = _silu(_layer_norm(acc, row(clg_ref), row(clb_ref)))
    nconv_ref[0:CONV_BUF - 1] = sconv_ref[1:CONV_BUF]
    nconv_ref[CONV_BUF - 1] = glu

    off_u = D_POOL + 2 * D_CONV
    u = _gelu_tanh(_dot(hb, win_ref[:, off_u:off_u + D_GMLP]))
    v = _gelu_tanh(_dot(hb, win_ref[:, off_u + D_GMLP:off_u + 2 * D_GMLP]))
    v = _layer_norm(v, row(glg_ref), row(glb_ref))
    vrow_ref[...] = v
    yc = u * (row(ws0_ref) * v + row(bs0_ref))

    ycat = jnp.concatenate([ya, yb, yc], axis=-1).astype(_BF16)
    x1 = x + mod_ref[2] * _dot(ycat, wout_ref[...])
    o_ref[...] = _ffn_body(x1, mod_ref[3], mod_ref[4], mod_ref[5], row(ng2_ref), w1_ref, w3_ref,
                           w2_ref, fg_ref[...], act, final_norm)


def _layer_sample_call(layer, xs, mod, P, W, state_pool, state_conv, final_norm):
    n = xs.shape[0]
    depth = P["norm1_g"].shape[0]
    sp = _param_specs(layer, depth)
    in_specs = [
        _resident((n, D_MODEL), (0, 0)),
        _resident((None, N_MOD, n, D_MODEL), (layer, 0, 0, 0)),
        sp["ng"], sp["ng"], sp["fg"], sp["w_in"], sp["pool_bd"], sp["rows256"], sp["conv_w"],
        sp["rows256"], sp["rows256"], sp["rows256"], sp["rows512"], sp["rows512"], sp["rows512"],
        sp["rows512"], sp["w_out"], sp["w_ff13"], sp["w_ff13"], sp["w_ff2"],
        _resident((None, POOL_BUF, n, D_POOL), (layer, 0, 0, 0)),
        _resident((None, CONV_BUF, n, D_CONV), (layer, 0, 0, 0)),
    ]
    out_shape = (
        jax.ShapeDtypeStruct((n, D_MODEL), _F32),
        jax.ShapeDtypeStruct((POOL_BUF, n, D_POOL), _F32),
        jax.ShapeDtypeStruct((CONV_BUF, n, D_CONV), _F32),
        jax.ShapeDtypeStruct((n, D_GMLP), _F32),
    )
    whole = lambda shape: pl.BlockSpec(shape, lambda i: (0,) * len(shape))
    out_specs = (whole((n, D_MODEL)), whole((POOL_BUF, n, D_POOL)), whole((CONV_BUF, n, D_CONV)),
                 whole((n, D_GMLP)))
    return pl.pallas_call(
        functools.partial(_layer_sample_kernel, layer=layer, final_norm=final_norm),
        out_shape=out_shape,
        grid=(1,),
        in_specs=in_specs,
        out_specs=out_specs,
        scratch_shapes=[pltpu.VMEM((n, D_FF), _BF16)],
        compiler_params=pltpu.CompilerParams(
            dimension_semantics=("arbitrary",), vmem_limit_bytes=VMEM_LIMIT),
        name="layer_sample",
    )(xs, mod, P["norm1_g"], P["norm2_g"], P["final_g"], W["w_in"], P["pool_bd"], P["pool_scale"],
      P["conv_w"], P["conv_b"], P["conv_ln_g"], P["conv_ln_b"], P["gmlp_ln_g"], P["gmlp_ln_b"],
      P["ws0"], P["bs0"], W["w_out"], W["w_ff1"], W["w_ff3"], W["w_ff2"], state_pool, state_conv)


def _prep_params(norm1_g, norm2_g, pool_w, pool_scale, conv_w, conv_b, conv_ln_g, conv_ln_b,
                 gmlp_ln_g, gmlp_ln_b, gmlp_ws, gmlp_bs, final_g):
    depth = pool_w.shape[0]
    n_grp = len(POOL_WINDOWS)
    eye = jnp.eye(n_grp, dtype=_F32)
    pool_bd = (eye[None, :, None, :, None] * pool_w[:, :, :, None, :]).reshape(depth, D_POOL, D_POOL)
    mask = jnp.tril(jnp.ones((CHUNK, CHUNK), dtype=bool))
    ws = jnp.where(mask, gmlp_ws, jnp.zeros_like(gmlp_ws))
    ws_pairs = ws.reshape(depth, N_GMLP_HEADS // 2, 2, CHUNK, CHUNK).transpose(0, 1, 3, 2, 4).reshape(
        depth, N_GMLP_HEADS // 2, CHUNK, 2 * CHUNK)
    bs_full = jnp.repeat(gmlp_bs.transpose(0, 2, 1), GMLP_HEAD, axis=2)
    ws0 = jnp.repeat(gmlp_ws[:, :, 0, 0], GMLP_HEAD, axis=1)
    bs0 = jnp.repeat(gmlp_bs[:, :, 0], GMLP_HEAD, axis=1)
    return dict(
        norm1_g=norm1_g, norm2_g=norm2_g, final_g=final_g.reshape(1, D_MODEL),
        pool_bd=pool_bd.astype(_BF16), pool_scale=pool_scale,
        conv_w=conv_w, conv_b=conv_b, conv_ln_g=conv_ln_g, conv_ln_b=conv_ln_b,
        gmlp_ln_g=gmlp_ln_g, gmlp_ln_b=gmlp_ln_b,
        ws_pairs=ws_pairs.astype(_BF16), bs_full=bs_full, ws0=ws0, bs0=bs0,
    )


def kernel(x_prompt, x_sample, c_prompt, c_sample, state_pool, state_conv, w_ada, b_ada, norm1_g, norm2_g, w_in, pool_w, pool_scale, conv_w, conv_b, conv_ln_g, conv_ln_b, gmlp_ln_g, gmlp_ln_b, gmlp_ws, gmlp_bs, w_out, w_ff1, w_ff3, w_ff2, final_g):
    depth = w_in.shape[0]
    nb = x_prompt.shape[0]
    ns = x_sample.shape[0]
    assert x_sample.shape[1] == 1 and x_prompt.shape[1] % TOKEN_TILE == 0 and ns % nb == 0

    c_all = jnp.concatenate([c_sample, c_prompt], axis=0)
    mod = _ada_call(c_all, w_ada, b_ada)
    P = _prep_params(norm1_g, norm2_g, pool_w, pool_scale, conv_w, conv_b, conv_ln_g, conv_ln_b,
                     gmlp_ln_g, gmlp_ln_b, gmlp_ws, gmlp_bs, final_g)
    big_f32 = dict(w_in=w_in, w_out=w_out, w_ff1=w_ff1, w_ff3=w_ff3, w_ff2=w_ff2)
    W = {name: w[0].astype(_BF16) for name, w in big_f32.items()}

    xp = x_prompt
    xs = x_sample.reshape(ns, D_MODEL)
    spool_t = state_pool.transpose(0, 2, 1, 3)
    sconv_t = state_conv.transpose(0, 2, 1, 3)
    pool_p, conv_p, pool_s, conv_s, v_s = [], [], [], [], []
    for l in range(depth):
        last = l == depth - 1
        xp, npool_p, nconv_p, next_w = _layer_prompt_call(
            l, xp, mod, ns, P, W, None if last else big_f32, last)
        xs, npool_s, nconv_s, nv_s = _layer_sample_call(l, xs, mod, P, W, spool_t, sconv_t, last)
        W = next_w
        pool_p.append(npool_p)
        conv_p.append(nconv_p)
        pool_s.append(npool_s)
        conv_s.append(nconv_s)
        v_s.append(nv_s.reshape(ns, 1, D_GMLP))
    return (xp, xs.reshape(ns, 1, D_MODEL), jnp.stack(pool_p), jnp.stack(conv_p),
            jnp.stack(pool_s).transpose(0, 2, 1, 3), jnp.stack(conv_s).transpose(0, 2, 1, 3),
            jnp.stack(v_s))
```

```python
import functools

import jax
import jax.numpy as jnp
from jax.experimental import pallas as pl
from jax.experimental.pallas import tpu as pltpu

D_MODEL = 1024
D_POOL = 256
POOL_WINDOWS = (2, 4, 8, 16)
POOL_GROUP = 64
POOL_BUF = 15
D_CONV = 256
CONV_WIDTH = 31
CONV_BUF = 30
D_GMLP = 512
GMLP_HEAD = 64
N_GMLP_HEADS = 8
CHUNK = 128
D_IN = D_POOL + 2 * D_CONV + 2 * D_GMLP
D_FF = 2816
N_MOD = 6
EPS = 1e-6

LANES = 128
SUBLANES = 8

TOKEN_TILE = 512
FF_CHUNK = 256
ADA_TERMS = 2
POOL_HIST = 16
CONV_HIST = 32
VMEM_LIMIT = 60 * 1024 * 1024

_BF16 = jnp.bfloat16
_F32 = jnp.float32


def _dot(a, b):
    return jnp.dot(a, b, preferred_element_type=_F32)


_LOG2E = 1.4426950408889634


def _sigmoid(x):
    return 1.0 / (1.0 + jnp.exp2(x * -_LOG2E))


def _silu(x):
    return x * _sigmoid(x)


def _gelu_tanh(x):
    k1 = -2.0 * 0.7978845608028654 * _LOG2E
    k3 = k1 * 0.044715
    return x / (1.0 + jnp.exp2(x * (k1 + k3 * (x * x))))


def _rms_norm(x, g):
    ms = jnp.mean(x * x, axis=-1, keepdims=True)
    return (x * jax.lax.rsqrt(ms + EPS)) * g


def _mod_norm(x, g, sc, sh):
    ms = jnp.mean(x * x, axis=-1, keepdims=True)
    return (x * jax.lax.rsqrt(ms + EPS)) * (g * (1.0 + sc)) + sh


def _layer_norm(x, g, b):
    mu = jnp.mean(x, axis=-1, keepdims=True)
    xc = x - mu
    var = jnp.mean(xc * xc, axis=-1, keepdims=True)
    return xc * jax.lax.rsqrt(var + EPS) * g + b


def _lo_lanes():
    return jax.lax.broadcasted_iota(jnp.int32, (1, LANES), 1) < POOL_GROUP


def _ada_kernel(c_ref, w_ref, b_ref, o_ref):
    l = pl.program_id(0)
    s = _silu(c_ref[...]).astype(_BF16)
    res = _dot(s, w_ref[...].astype(_BF16)) + b_ref[pl.ds(l, 1), :]
    for k in range(ADA_TERMS):
        o_ref[k] = res[:, k * D_MODEL:(k + 1) * D_MODEL]


def _ada_call(c_all, w_ada, b_ada):
    depth = w_ada.shape[0]
    n = c_all.shape[0]
    cols = ADA_TERMS * D_MODEL
    return pl.pallas_call(
        _ada_kernel,
        out_shape=jax.ShapeDtypeStruct((depth, N_MOD, n, D_MODEL), _F32),
        grid=(depth, N_MOD // ADA_TERMS),
        in_specs=[
            pl.BlockSpec((n, D_MODEL), lambda l, j: (0, 0)),
            pl.BlockSpec((None, D_MODEL, cols), lambda l, j: (l, 0, j)),
            pl.BlockSpec((depth, cols), lambda l, j: (0, j)),
        ],
        out_specs=pl.BlockSpec((None, ADA_TERMS, n, D_MODEL), lambda l, j: (l, j, 0, 0)),
        compiler_params=pltpu.CompilerParams(
            dimension_semantics=("arbitrary", "arbitrary"), vmem_limit_bytes=VMEM_LIMIT),
        name="ada_mod",
    )(c_all, w_ada, b_ada)


def _pool_mix(means, xa, poolw_ref, pools):
    d = jnp.concatenate(means, axis=-1) - xa
    return _dot(d.astype(_BF16), poolw_ref[...]) * pools


def _ffn_body(x, sh, sc, g, ng, w1_ref, w3_ref, w2_ref, fg, act, final_norm):
    h = _mod_norm(x, ng, sc, sh)
    hb = h.astype(_BF16)
    for c in range(D_FF // FF_CHUNK):
        cols = slice(c * FF_CHUNK, (c + 1) * FF_CHUNK)
        a1 = _dot(hb, w1_ref[:, cols])
        a3 = _dot(hb, w3_ref[:, cols])
        act[:, cols] = (_silu(a1) * a3).astype(_BF16)
    ff = _dot(act[...], w2_ref[...])
    out = x + g * ff
    if final_norm:
        out = _rms_norm(out, fg)
    return out


def _mixer_prompt_stages(x_ref, sh, sc, g, ng, t, win_ref, poolw_ref, pools, convw_ref, convb, clg, clb,
                         glg, glb, wsp_ref, bsf_ref, wout_ref, xa_ext, glu_ext, ycat, finish):
    T = TOKEN_TILE
    x = x_ref[...]
    h = _mod_norm(x, ng, sc, sh)
    hb = h.astype(_BF16)
    lo = _lo_lanes()
    yield

    xa = _dot(hb, win_ref[:, 0:D_POOL])
    a = _dot(hb, win_ref[:, D_POOL:D_POOL + D_CONV])
    gt = _dot(hb, win_ref[:, D_POOL + D_CONV:D_POOL + 2 * D_CONV])
    yield
    xa_ext[POOL_HIST:POOL_HIST + T, :] = xa
    glu_ext[CONV_HIST:CONV_HIST + T, :] = a * _sigmoid(gt)
    pos1 =(t * T + 1 + jax.lax.broadcasted_iota(jnp.int32, (T, LANES), 0)).astype(_F32)
    means = []
    for half in range(2):
        w_a, w_b = POOL_WINDOWS[2 * half], POOL_WINDOWS[2 * half + 1]
        cur = xa_ext[:, half * LANES:(half + 1) * LANES]
        sums = {1: cur}
        w = 1
        while w < w_b:
            cur = cur + pltpu.roll(cur, w, axis=0)
            w *= 2
            sums[w] = cur
        s_a = sums[w_a][POOL_HIST:POOL_HIST + T, :]
        s_b = sums[w_b][POOL_HIST:POOL_HIST + T, :]
        cnt = jnp.minimum(pos1, jnp.where(lo, float(w_a), float(w_b)))
        means.append(jnp.where(lo, s_a, s_b) / cnt)
    ya = _pool_mix(means, xa, poolw_ref, pools)
    ycat[:, 0:D_POOL] = ya.astype(_BF16)
    yield

    off_u = D_POOL + 2 * D_CONV
    u = _gelu_tanh(_dot(hb, win_ref[:, off_u:off_u + D_GMLP]))
    v = _gelu_tanh(_dot(hb, win_ref[:, off_u + D_GMLP:off_u + 2 * D_GMLP]))
    v = _layer_norm(v, glg, glb)
    yield

    lead = CONV_HIST - CONV_BUF
    acc = convb
    for r in range(SUBLANES):
        part = None
        for q in range((lead + CONV_WIDTH - 1) // SUBLANES + 1):
            j = SUBLANES * q + r
            if lead <= j < lead + CONV_WIDTH:
                term = convw_ref[j - lead:j - lead + 1, :] * glu_ext[SUBLANES * q:SUBLANES * q + T + SUBLANES, :]
                part = term if part is None else part + term
        acc = acc + part[r:r + T, :]
    yb = _silu(_layer_norm(acc, clg, clb))
    ycat[:, D_POOL:D_POOL + D_CONV] = yb.astype(_BF16)
    yield

    zero = jnp.zeros((CHUNK, LANES), _F32)
    for cp in range(T // (2 * CHUNK)):
        r0 = cp * 2 * CHUNK
        for m in range(N_GMLP_HEADS // 2):
            l0 = m * LANES
            v0 = v[r0:r0 + CHUNK, l0:l0 + LANES]
            v1 = v[r0 + CHUNK:r0 + 2 * CHUNK, l0:l0 + LANES]
            top = jnp.concatenate([jnp.where(lo, v0, zero), jnp.where(lo, v1, zero)], axis=1)
            bot = jnp.concatenate([jnp.where(lo, zero, v0), jnp.where(lo, zero, v1)], axis=1)
            rhs = jnp.concatenate([top, bot], axis=0).astype(_BF16)
            z = _dot(wsp_ref[m], rhs)
            bias = bsf_ref[:, l0:l0 + LANES]
            y0 = u[r0:r0 + CHUNK, l0:l0 + LANES] * (z[:, 0:LANES] + bias)
            y1 = u[r0 + CHUNK:r0 + 2 * CHUNK, l0:l0 + LANES] * (z[:, LANES:2 * LANES] + bias)
            c0 = D_POOL + D_CONV + l0
            ycat[r0:r0 + CHUNK, c0:c0 + LANES] = y0.astype(_BF16)
            ycat[r0 + CHUNK:r0 + 2 * CHUNK, c0:c0 + LANES] = y1.astype(_BF16)
    yield

    y = _dot(ycat[...], wout_ref[...])
    finish(x + g * y)
    xa_ext[0:POOL_HIST, :] = xa_ext[T:T + POOL_HIST, :]
    glu_ext[0:CONV_HIST, :] = glu_ext[T:T + CONV_HIST, :]
    yield


def _ffn_prompt_stages(o_ref, x1_buf, h2b, g, w1_ref, w3_ref, w2_ref, fg, act, final_norm):
    o_ref[...] = x1_buf[...]
    yield
    for c in range(D_FF // FF_CHUNK):
        cols = slice(c * FF_CHUNK, (c + 1) * FF_CHUNK)
        a1 = _dot(h2b[...], w1_ref[:, cols])
        a3 = _dot(h2b[...], w3_ref[:, cols])
        act[:, cols] = (_silu(a1) * a3).astype(_BF16)
        yield
    out = o_ref[...] + g * _dot(act[...], w2_ref[...])
    if final_norm:
        out = _rms_norm(out, fg)
    o_ref[...] = out
    yield


_STAGE_ORDER = "ba" + "bba" + "bbaa" + "bbbbb" + "aa" + "bba" + "b"
_CAST_AFTER_STAGE = 3
_N_PROMPT_INPUTS = 20
_BIG_WEIGHTS = ("w_in", "w_out", "w_ff1", "w_ff3", "w_ff2")


def _layer_prompt_kernel(*refs, layer, n_t, n_tiles, final_norm, n_cast):
    n_in = _N_PROMPT_INPUTS + n_cast
    (x_ref, mod_ref, ng1_ref, ng2_ref, fg_ref, win_ref, poolw_ref, pools_ref, convw_ref, convb_ref,
     clg_ref, clb_ref, glg_ref, glb_ref, wsp_ref, bsf_ref, wout_ref, w1_ref, w3_ref,
     w2_ref) = refs[:_N_PROMPT_INPUTS]
    cast_src = refs[_N_PROMPT_INPUTS:n_in]
    o_ref, npool_ref, nconv_ref = refs[n_in:n_in + 3]
    cast_dst = refs[n_in + 3:n_in + 3 + n_cast]
    xa_ext, glu_ext, ycat, act, x1_buf, h2b = refs[n_in + 3 + n_cast:]
    T = TOKEN_TILE
    s = pl.program_id(0)
    tile_a = jnp.minimum(s, n_tiles - 1)
    n_a = tile_a // n_t
    t_a = tile_a - n_a * n_t
    n_b = jnp.maximum(s - 1, 0) // n_t
    row = lambda ref: ref[layer:layer + 1, :]
    mod = lambda k, n: mod_ref[k, pl.ds(n, 1), :]

    @pl.when(t_a == 0)
    def _():
        xa_ext[0:POOL_HIST, :] = jnp.zeros((POOL_HIST, D_POOL), _F32)
        glu_ext[0:CONV_HIST, :] = jnp.zeros((CONV_HIST, D_CONV), _F32)
        glu_ext[CONV_HIST + T:CONV_HIST + T + SUBLANES, :] = jnp.zeros((SUBLANES, D_CONV), _F32)

    def finish_mixer(x1):
        x1_buf[...] = x1
        h2 = _mod_norm(x1, row(ng2_ref), mod(4, n_a), mod(3, n_a))
        h2b[...] = h2.astype(_BF16)

    def run(order):
        stages = {
            "a": _mixer_prompt_stages(
                x_ref, mod(0, n_a), mod(1, n_a), mod(2, n_a), row(ng1_ref), t_a, win_ref, poolw_ref,
                row(pools_ref), convw_ref, row(convb_ref), row(clg_ref), row(clb_ref), row(glg_ref),
                row(glb_ref), wsp_ref, bsf_ref, wout_ref, xa_ext, glu_ext, ycat, finish_mixer),
            "b": _ffn_prompt_stages(o_ref, x1_buf, h2b, mod(5, n_b), w1_ref, w3_ref, w2_ref,
                                    fg_ref[...], act, final_norm),
        }
        for i, which in enumerate(order):
            next(stages[which])
            if i == _CAST_AFTER_STAGE:
                for src, dst in zip(cast_src, cast_dst):
                    dst[...] = src[...].astype(_BF16)
        for which in set(order):
            assert next(stages[which], "done") == "done"

    pl.when(s == 0)(lambda: run(_STAGE_ORDER.replace("b", "")))
    pl.when(jnp.logical_and(s > 0, s < n_tiles))(lambda: run(_STAGE_ORDER))
    pl.when(s == n_tiles)(lambda: run(_STAGE_ORDER.replace("a", "")))

    @pl.when(jnp.logical_and(t_a == n_t - 1, s < n_tiles))
    def _():
        npool_ref[...] = xa_ext[pl.ds(POOL_HIST + T - POOL_BUF, POOL_BUF), :]
        nconv_ref[...] = glu_ext[pl.ds(CONV_HIST + T - CONV_BUF, CONV_BUF), :]


def _resident(shape, index):
    return pl.BlockSpec(shape, lambda *_: index, pipeline_mode=pl.Buffered(1))


def _param_specs(layer, depth):
    per_layer_rows = _resident((depth, D_MODEL), (0, 0))
    return dict(
        ng=per_layer_rows,
        fg=_resident((1, D_MODEL), (0, 0)),
        w_in=_resident((D_MODEL, D_IN), (0, 0)),
        pool_bd=_resident((None, D_POOL, D_POOL), (layer, 0, 0)),
        rows256=_resident((depth, D_POOL), (0, 0)),
        conv_w=_resident((None, CONV_WIDTH, D_CONV), (layer, 0, 0)),
        rows512=_resident((depth, D_GMLP), (0, 0)),
        ws_pairs=_resident((None, N_GMLP_HEADS // 2, CHUNK, 2 * CHUNK), (layer, 0, 0, 0)),
        bs_full=_resident((None, CHUNK, D_GMLP), (layer, 0, 0)),
        w_out=_resident((D_MODEL, D_MODEL), (0, 0)),
        w_ff13=_resident((D_MODEL, D_FF), (0, 0)),
        w_ff2=_resident((D_FF, D_MODEL), (0, 0)),
    )


def _cast_rows(n_rows, n_steps):
    bf16_rows = 2 * SUBLANES
    for rows in range(bf16_rows, n_rows + 1, bf16_rows):
        if n_rows % rows == 0 and n_rows // rows <= n_steps:
            return rows
    raise ValueError((n_rows, n_steps))


def _layer_prompt_call(layer, x, mod, n_sample, P, W, next_f32, final_norm):
    nb, seq, _ = x.shape
    depth = P["norm1_g"].shape[0]
    T = TOKEN_TILE
    n_t = seq // T
    n_tiles = nb * n_t
    n_steps = n_tiles + 1
    sp = _param_specs(layer, depth)
    cast_in, cast_specs_in, cast_shapes, cast_specs_out = [], [], [], []
    if next_f32 is not None:
        for name in _BIG_WEIGHTS:
            w = next_f32[name]
            n_rows, n_cols = w.shape[1:]
            rows = _cast_rows(n_rows, n_steps)
            last = n_rows // rows - 1
            cast_in.append(w)
            cast_specs_in.append(pl.BlockSpec(
                (None, rows, n_cols), lambda s, last=last: (layer + 1, jnp.minimum(s, last), 0)))
            cast_shapes.append(jax.ShapeDtypeStruct((n_rows, n_cols), _BF16))
            cast_specs_out.append(pl.BlockSpec(
                (rows, n_cols), lambda s, last=last: (jnp.minimum(s, last), 0)))

    def x_map(s):
        tile = jnp.minimum(s, n_tiles - 1)
        return (tile // n_t, tile % n_t, 0)

    def o_map(s):
        tile = jnp.maximum(s - 1, 0)
        return (tile // n_t, tile % n_t, 0)

    def state_map(s):
        return (jnp.minimum(s, n_tiles - 1) // n_t, 0, 0)

    in_specs = [
        pl.BlockSpec((None, T, D_MODEL), x_map),
        pl.BlockSpec((None, N_MOD, nb, D_MODEL), lambda s: (layer, 0, n_sample // nb, 0),
                     pipeline_mode=pl.Buffered(1)),
        sp["ng"], sp["ng"], sp["fg"], sp["w_in"], sp["pool_bd"], sp["rows256"], sp["conv_w"],
        sp["rows256"], sp["rows256"], sp["rows256"], sp["rows512"], sp["rows512"], sp["ws_pairs"],
        sp["bs_full"], sp["w_out"], sp["w_ff13"], sp["w_ff13"], sp["w_ff2"],
    ]
    assert len(in_specs) == _N_PROMPT_INPUTS
    out_shape = [
        jax.ShapeDtypeStruct((nb, seq, D_MODEL), _F32),
        jax.ShapeDtypeStruct((nb, POOL_BUF, D_POOL), _F32),
        jax.ShapeDtypeStruct((nb, CONV_BUF, D_CONV), _F32),
    ]
    out_specs = [
        pl.BlockSpec((None, T, D_MODEL), o_map),
        pl.BlockSpec((None, POOL_BUF, D_POOL), state_map),
        pl.BlockSpec((None, CONV_BUF, D_CONV), state_map),
    ]
    outs = pl.pallas_call(
        functools.partial(_layer_prompt_kernel, layer=layer, n_t=n_t, n_tiles=n_tiles,
                          final_norm=final_norm, n_cast=len(cast_in)),
        out_shape=out_shape + cast_shapes,
        grid=(n_steps,),
        in_specs=in_specs + cast_specs_in,
        out_specs=out_specs + cast_specs_out,
        scratch_shapes=[
            pltpu.VMEM((T + POOL_HIST, D_POOL), _F32),
            pltpu.VMEM((T + CONV_HIST + SUBLANES, D_CONV), _F32),
            pltpu.VMEM((T, D_MODEL), _BF16),
            pltpu.VMEM((T, D_FF), _BF16),
            pltpu.VMEM((T, D_MODEL), _F32),
            pltpu.VMEM((T, D_MODEL), _BF16),
        ],
        compiler_params=pltpu.CompilerParams(
            dimension_semantics=("arbitrary",), vmem_limit_bytes=VMEM_LIMIT),
        name="layer_prompt",
    )(x, mod, P["norm1_g"], P["norm2_g"], P["final_g"], W["w_in"], P["pool_bd"], P["pool_scale"],
      P["conv_w"], P["conv_b"], P["conv_ln_g"], P["conv_ln_b"], P["gmlp_ln_g"], P["gmlp_ln_b"],
      P["ws_pairs"], P["bs_full"], W["w_out"], W["w_ff1"], W["w_ff3"], W["w_ff2"], *cast_in)
    next_w = dict(zip(_BIG_WEIGHTS, outs[3:])) if cast_in else None
    return outs[0], outs[1], outs[2], next_w


def _layer_sample_kernel(x_ref, mod_ref, ng1_ref, ng2_ref, fg_ref, win_ref, poolw_ref, pools_ref,
                         convw_ref, convb_ref, clg_ref, clb_ref, glg_ref, glb_ref, ws0_ref,
                         bs0_ref, wout_ref, w1_ref, w3_ref, w2_ref, spool_ref, sconv_ref,
                         o_ref, npool_ref, nconv_ref, vrow_ref, act, *, layer, final_norm):
    row = lambda ref: ref[layer:layer + 1, :]
    x = x_ref[...]
    h = _mod_norm(x, row(ng1_ref), mod_ref[1], mod_ref[0])
    hb = h.astype(_BF16)
    lo = _lo_lanes()

    xa = _dot(hb, win_ref[:, 0:D_POOL])
    means = []
    for half in range(2):
        cols = slice(half * LANES, (half + 1) * LANES)
        w_a, w_b = POOL_WINDOWS[2 * half], POOL_WINDOWS[2 * half + 1]
        s = xa[:, cols]
        for j in range(1, w_a):
            s = s + spool_ref[POOL_BUF - j, :, cols]
        s_a = s
        for j in range(w_a, w_b):
            s = s + spool_ref[POOL_BUF - j, :, cols]
        means.append(jnp.where(lo, s_a, s) / jnp.where(lo, float(w_a), float(w_b)))
    ya = _pool_mix(means, xa, poolw_ref, row(pools_ref))
    npool_ref[0:POOL_BUF - 1] = spool_ref[1:POOL_BUF]
    npool_ref[POOL_BUF - 1] = xa

    a = _dot(hb, win_ref[:, D_POOL:D_POOL + D_CONV])
    gt = _dot(hb, win_ref[:, D_POOL + D_CONV:D_POOL + 2 * D_CONV])
    glu = a * _sigmoid(gt)
    acc = row(convb_ref) + convw_ref[CONV_BUF:CONV_WIDTH, :] * glu
    for r in range(CONV_BUF):
        acc = acc + convw_ref[r:r + 1, :] * sconv_ref[r]
    yb = _silu(_layer_norm(acc, row(clg_ref), row(clb_ref)))
    nconv_ref[0:CONV_BUF - 1] = sconv_ref[1:CONV_BUF]
    nconv_ref[CONV_BUF - 1] = glu

    off_u = D_POOL + 2 * D_CONV
    u = _gelu_tanh(_dot(hb, win_ref[:, off_u:off_u + D_GMLP]))
    v = _gelu_tanh(_dot(hb, win_ref[:, off_u + D_GMLP:off_u + 2 * D_GMLP]))
    v = _layer_norm(v, row(glg_ref), row(glb_ref))
    vrow_ref[...] = v
    yc = u * (row(ws0_ref) * v + row(bs0_ref))

    ycat = jnp.concatenate([ya, yb, yc], axis=-1).astype(_BF16)
    x1 = x + mod_ref[2] * _dot(ycat, wout_ref[...])
    o_ref[...] = _ffn_body(x1, mod_ref[3], mod_ref[4], mod_ref[5], row(ng2_ref), w1_ref, w3_ref,
                           w2_ref, fg_ref[...], act, final_norm)


def _layer_sample_call(layer, xs, mod, P, W, state_pool, state_conv, final_norm):
    n = xs.shape[0]
    depth = P["norm1_g"].shape[0]
    sp = _param_specs(layer, depth)
    in_specs = [
        _resident((n, D_MODEL), (0, 0)),
        _resident((None, N_MOD, n, D_MODEL), (layer, 0, 0, 0)),
        sp["ng"], sp["ng"], sp["fg"], sp["w_in"], sp["pool_bd"], sp["rows256"], sp["conv_w"],
        sp["rows256"], sp["rows256"], sp["rows256"], sp["rows512"], sp["rows512"], sp["rows512"],
        sp["rows512"], sp["w_out"], sp["w_ff13"], sp["w_ff13"], sp["w_ff2"],
        _resident((None, POOL_BUF, n, D_POOL), (layer, 0, 0, 0)),
        _resident((None, CONV_BUF, n, D_CONV), (layer, 0, 0, 0)),
    ]
    out_shape = (
        jax.ShapeDtypeStruct((n, D_MODEL), _F32),
        jax.ShapeDtypeStruct((POOL_BUF, n, D_POOL), _F32),
        jax.ShapeDtypeStruct((CONV_BUF, n, D_CONV), _F32),
        jax.ShapeDtypeStruct((n, D_GMLP), _F32),
    )
    whole = lambda shape: pl.BlockSpec(shape, lambda i: (0,) * len(shape))
    out_specs = (whole((n, D_MODEL)), whole((POOL_BUF, n, D_POOL)), whole((CONV_BUF, n, D_CONV)),
                 whole((n, D_GMLP)))
    return pl.pallas_call(
        functools.partial(_layer_sample_kernel, layer=layer, final_norm=final_norm),
        out_shape=out_shape,
        grid=(1,),
        in_specs=in_specs,
        out_specs=out_specs,
        scratch_shapes=[pltpu.VMEM((n, D_FF), _BF16)],
        compiler_params=pltpu.CompilerParams(
            dimension_semantics=("arbitrary",), vmem_limit_bytes=VMEM_LIMIT),
        name="layer_sample",
    )(xs, mod, P["norm1_g"], P["norm2_g"], P["final_g"], W["w_in"], P["pool_bd"], P["pool_scale"],
      P["conv_w"], P["conv_b"], P["conv_ln_g"], P["conv_ln_b"], P["gmlp_ln_g"], P["gmlp_ln_b"],
      P["ws0"], P["bs0"], W["w_out"], W["w_ff1"], W["w_ff3"], W["w_ff2"], state_pool, state_conv)


def _prep_params(norm1_g, norm2_g, pool_w, pool_scale, conv_w, conv_b, conv_ln_g, conv_ln_b,
                 gmlp_ln_g, gmlp_ln_b, gmlp_ws, gmlp_bs, final_g):
    depth = pool_w.shape[0]
    n_grp = len(POOL_WINDOWS)
    eye = jnp.eye(n_grp, dtype=_F32)
    pool_bd = (eye[None, :, None, :, None] * pool_w[:, :, :, None, :]).reshape(depth, D_POOL, D_POOL)
    mask = jnp.tril(jnp.ones((CHUNK, CHUNK), dtype=bool))
    ws = jnp.where(mask, gmlp_ws, jnp.zeros_like(gmlp_ws))
    ws_pairs = ws.reshape(depth, N_GMLP_HEADS // 2, 2, CHUNK, CHUNK).transpose(0, 1, 3, 2, 4).reshape(
        depth, N_GMLP_HEADS // 2, CHUNK, 2 * CHUNK)
    bs_full = jnp.repeat(gmlp_bs.transpose(0, 2, 1), GMLP_HEAD, axis=2)
    ws0 = jnp.repeat(gmlp_ws[:, :, 0, 0], GMLP_HEAD, axis=1)
    bs0 = jnp.repeat(gmlp_bs[:, :, 0], GMLP_HEAD, axis=1)
    return dict(
        norm1_g=norm1_g, norm2_g=norm2_g, final_g=final_g.reshape(1, D_MODEL),
        pool_bd=pool_bd.astype(_BF16), pool_scale=pool_scale,
        conv_w=conv_w, conv_b=conv_b, conv_ln_g=conv_ln_g, conv_ln_b=conv_ln_b,
        gmlp_ln_g=gmlp_ln_g, gmlp_ln_b=gmlp_ln_b,
        ws_pairs=ws_pairs.astype(_BF16), bs_full=bs_full, ws0=ws0, bs0=bs0,
    )


def kernel(x_prompt, x_sample, c_prompt, c_sample, state_pool, state_conv, w_ada, b_ada, norm1_g, norm2_g, w_in, pool_w, pool_scale, conv_w, conv_b, conv_ln_g, conv_ln_b, gmlp_ln_g, gmlp_ln_b, gmlp_ws, gmlp_bs, w_out, w_ff1, w_ff3, w_ff2, final_g):
    depth = w_in.shape[0]
    nb = x_prompt.shape[0]
    ns = x_sample.shape[0]
    assert x_sample.shape[1] == 1 and x_prompt.shape[1] % TOKEN_TILE == 0 and ns % nb == 0

    c_all = jnp.concatenate([c_sample, c_prompt], axis=0)
    mod = _ada_call(c_all, w_ada, b_ada)
    P = _prep_params(norm1_g, norm2_g, pool_w, pool_scale, conv_w, conv_b, conv_ln_g, conv_ln_b,
                     gmlp_ln_g, gmlp_ln_b, gmlp_ws, gmlp_bs, final_g)
    big_f32 = dict(w_in=w_in, w_out=w_out, w_ff1=w_ff1, w_ff3=w_ff3, w_ff2=w_ff2)
    W = {name: w[0].astype(_BF16) for name, w in big_f32.items()}

    xp = x_prompt
    xs = x_sample.reshape(ns, D_MODEL)
    spool_t = state_pool.transpose(0, 2, 1, 3)
    sconv_t = state_conv.transpose(0, 2, 1, 3)
    pool_p, conv_p, pool_s, conv_s, v_s = [], [], [], [], []
    for l in range(depth):
        last = l == depth - 1
        xp, npool_p, nconv_p, next_w = _layer_prompt_call(
            l, xp, mod, ns, P, W, None if last else big_f32, last)
        xs, npool_s, nconv_s, nv_s = _layer_sample_call(l, xs, mod, P, W, spool_t, sconv_t, last)
        W = next_w
        pool_p.append(npool_p)
        conv_p.append(nconv_p)
        pool_s.append(npool_s)
        conv_s.append(nconv_s)
        v_s.append(nv_s.reshape(ns, 1, D_GMLP))
    return (xp, xs.reshape(ns, 1, D_MODEL), jnp.stack(pool_p), jnp.stack(conv_p),
            jnp.stack(pool_s).transpose(0, 2, 1, 3), jnp.stack(conv_s).transpose(0, 2, 1, 3),
            jnp.stack(v_s))
```

```python
import functools

import jax
import jax.numpy as jnp
from jax.experimental import pallas as pl
from jax.experimental.pallas import tpu as pltpu

D_MODEL = 1024
D_POOL = 256
POOL_WINDOWS = (2, 4, 8, 16)
POOL_GROUP = 64
POOL_BUF = 15
D_CONV = 256
CONV_WIDTH = 31
CONV_BUF = 30
D_GMLP = 512
GMLP_HEAD = 64
N_GMLP_HEADS = 8
CHUNK = 128
D_IN = D_POOL + 2 * D_CONV + 2 * D_GMLP
D_FF = 2816
N_MOD = 6
EPS = 1e-6

LANES = 128
SUBLANES = 8

TOKEN_TILE = 512
FF_CHUNK = 256
ADA_TERMS = 2
POOL_HIST = 16
CONV_HIST = 32
VMEM_LIMIT = 60 * 1024 * 1024

_BF16 = jnp.bfloat16
_F32 = jnp.float32


def _dot(a, b):
    return jnp.dot(a, b, preferred_element_type=_F32)


_LOG2E = 1.4426950408889634


def _sigmoid(x):
    return 1.0 / (1.0 + jnp.exp2(x * -_LOG2E))


def _silu(x):
    return x * _sigmoid(x)


def _gelu_tanh(x):
    k1 = -2.0 * 0.7978845608028654 * _LOG2E
    k3 = k1 * 0.044715
    return x / (1.0 + jnp.exp2(x * (k1 + k3 * (x * x))))


def _rms_norm(x, g):
    ms = jnp.mean(x * x, axis=-1, keepdims=True)
    return (x * jax.lax.rsqrt(ms + EPS)) * g


def _mod_norm(x, g, sc, sh):
    ms = jnp.mean(x * x, axis=-1, keepdims=True)
    return (x * jax.lax.rsqrt(ms + EPS)) * (g * (1.0 + sc)) + sh


def _layer_norm(x, g, b):
    mu = jnp.mean(x, axis=-1, keepdims=True)
    xc = x - mu
    var = jnp.mean(xc * xc, axis=-1, keepdims=True)
    return xc * jax.lax.rsqrt(var + EPS) * g + b


def _lo_lanes():
    return jax.lax.broadcasted_iota(jnp.int32, (1, LANES), 1) < POOL_GROUP


def _ada_kernel(c_ref, w_ref, b_ref, o_ref):
    l = pl.program_id(0)
    s = _silu(c_ref[...]).astype(_BF16)
    res = _dot(s, w_ref[...].astype(_BF16)) + b_ref[pl.ds(l, 1), :]
    for k in range(ADA_TERMS):
        o_ref[k] = res[:, k * D_MODEL:(k + 1) * D_MODEL]


def _ada_call(c_all, w_ada, b_ada):
    depth = w_ada.shape[0]
    n = c_all.shape[0]
    cols = ADA_TERMS * D_MODEL
    return pl.pallas_call(
        _ada_kernel,
        out_shape=jax.ShapeDtypeStruct((depth, N_MOD, n, D_MODEL), _F32),
        grid=(depth, N_MOD // ADA_TERMS),
        in_specs=[
            pl.BlockSpec((n, D_MODEL), lambda l, j: (0, 0)),
            pl.BlockSpec((None, D_MODEL, cols), lambda l, j: (l, 0, j)),
            pl.BlockSpec((depth, cols), lambda l, j: (0, j)),
        ],
        out_specs=pl.BlockSpec((None, ADA_TERMS, n, D_MODEL), lambda l, j: (l, j, 0, 0)),
        compiler_params=pltpu.CompilerParams(
            dimension_semantics=("arbitrary", "arbitrary"), vmem_limit_bytes=VMEM_LIMIT),
        name="ada_mod",
    )(c_all, w_ada, b_ada)


def _pool_mix(means, xa, poolw_ref, pools):
    d = jnp.concatenate(means, axis=-1) - xa
    return _dot(d.astype(_BF16), poolw_ref[...]) * pools


def _mixer_prompt_stages(x_ref, sh, sc, g, ng, t, win_ref, poolw_ref, pools, convw_ref, convb, clg, clb,
                         glg, glb, wsp_ref, bsf_ref, wout_ref, xa_ext, glu_ext, ycat, finish):
    T = TOKEN_TILE
    x = x_ref[...]
    h = _mod_norm(x, ng, sc, sh)
    hb = h.astype(_BF16)
    lo = _lo_lanes()
    yield

    xa = _dot(hb, win_ref[:, 0:D_POOL])
    a = _dot(hb, win_ref[:, D_POOL:D_POOL + D_CONV])
    gt = _dot(hb, win_ref[:, D_POOL + D_CONV:D_POOL + 2 * D_CONV])
    yield
    xa_ext[POOL_HIST:POOL_HIST + T, :] = xa
    glu_ext[CONV_HIST:CONV_HIST + T, :] = a * _sigmoid(gt)
    pos1 =(t * T + 1 + jax.lax.broadcasted_iota(jnp.int32, (T, LANES), 0)).astype(_F32)
    means = []
    for half in range(2):
        w_a, w_b = POOL_WINDOWS[2 * half], POOL_WINDOWS[2 * half + 1]
        cur = xa_ext[:, half * LANES:(half + 1) * LANES]
        sums = {1: cur}
        w = 1
        while w < w_b:
            cur = cur + pltpu.roll(cur, w, axis=0)
            w *= 2
            sums[w] = cur
        s_a = sums[w_a][POOL_HIST:POOL_HIST + T, :]
        s_b = sums[w_b][POOL_HIST:POOL_HIST + T, :]
        cnt = jnp.minimum(pos1, jnp.where(lo, float(w_a), float(w_b)))
        means.append(jnp.where(lo, s_a, s_b) / cnt)
    ya = _pool_mix(means, xa, poolw_ref, pools)
    ycat[:, 0:D_POOL] = ya.astype(_BF16)
    yield

    off_u = D_POOL + 2 * D_CONV
    u = _gelu_tanh(_dot(hb, win_ref[:, off_u:off_u + D_GMLP]))
    v = _gelu_tanh(_dot(hb, win_ref[:, off_u + D_GMLP:off_u + 2 * D_GMLP]))
    v = _layer_norm(v, glg, glb)
    yield

    lead = CONV_HIST - CONV_BUF
    acc = convb
    for r in range(SUBLANES):
        part = None
        for q in range((lead + CONV_WIDTH - 1) // SUBLANES + 1):
            j = SUBLANES * q + r
            if lead <= j < lead + CONV_WIDTH:
                term = convw_ref[j - lead:j - lead + 1, :] * glu_ext[SUBLANES * q:SUBLANES * q + T + SUBLANES, :]
                part = term if part is None else part + term
        acc = acc + part[r:r + T, :]
    yb = _silu(_layer_norm(acc, clg, clb))
    ycat[:, D_POOL:D_POOL + D_CONV] = yb.astype(_BF16)
    yield

    zero = jnp.zeros((CHUNK, LANES), _F32)
    for cp in range(T // (2 * CHUNK)):
        r0 = cp * 2 * CHUNK
        for m in range(N_GMLP_HEADS // 2):
            l0 = m * LANES
            v0 = v[r0:r0 + CHUNK, l0:l0 + LANES]
            v1 = v[r0 + CHUNK:r0 + 2 * CHUNK, l0:l0 + LANES]
            top = jnp.concatenate([jnp.where(lo, v0, zero), jnp.where(lo, v1, zero)], axis=1)
            bot = jnp.concatenate([jnp.where(lo, zero, v0), jnp.where(lo, zero, v1)], axis=1)
            rhs = jnp.concatenate([top, bot], axis=0).astype(_BF16)
            z = _dot(wsp_ref[m], rhs)
            bias = bsf_ref[:, l0:l0 + LANES]
            y0 = u[r0:r0 + CHUNK, l0:l0 + LANES] * (z[:, 0:LANES] + bias)
            y1 = u[r0 + CHUNK:r0 + 2 * CHUNK, l0:l0 + LANES] * (z[:, LANES:2 * LANES] + bias)
            c0 = D_POOL + D_CONV + l0
            ycat[r0:r0 + CHUNK, c0:c0 + LANES] = y0.astype(_BF16)
            ycat[r0 + CHUNK:r0 + 2 * CHUNK, c0:c0 + LANES] = y1.astype(_BF16)
    yield

    y = _dot(ycat[...], wout_ref[...])
    finish(x + g * y)
    xa_ext[0:POOL_HIST, :] = xa_ext[T:T + POOL_HIST, :]
    glu_ext[0:CONV_HIST, :] = glu_ext[T:T + CONV_HIST, :]
    yield


def _ffn_prompt_stages(o_ref, x1_buf, h2b, g, w1_ref, w3_ref, w2_ref, fg, act, final_norm):
    o_ref[...] = x1_buf[...]
    yield
    for c in range(D_FF // FF_CHUNK):
        cols = slice(c * FF_CHUNK, (c + 1) * FF_CHUNK)
        a1 = _dot(h2b[...], w1_ref[:, cols])
        a3 = _dot(h2b[...], w3_ref[:, cols])
        act[:, cols] = (_silu(a1) * a3).astype(_BF16)
        yield
    out = o_ref[...] + g * _dot(act[...], w2_ref[...])
    if final_norm:
        out = _rms_norm(out, fg)
    o_ref[...] = out
    yield


_STAGE_ORDER = "ba" + "bba" + "bbaa" + "bbbbb" + "aa" + "bba" + "b"
_CAST_AFTER_STAGE = 3
_N_PROMPT_INPUTS = 20
_BIG_WEIGHTS = ("w_in", "w_out", "w_ff1", "w_ff3", "w_ff2")


def _layer_prompt_kernel(*refs, layer, n_t, n_tiles, final_norm, n_cast):
    n_in = _N_PROMPT_INPUTS + n_cast
    (x_ref, mod_ref, ng1_ref, ng2_ref, fg_ref, win_ref, poolw_ref, pools_ref, convw_ref, convb_ref,
     clg_ref, clb_ref, glg_ref, glb_ref, wsp_ref, bsf_ref, wout_ref, w1_ref, w3_ref,
     w2_ref) = refs[:_N_PROMPT_INPUTS]
    cast_src = refs[_N_PROMPT_INPUTS:n_in]
    o_ref, npool_ref, nconv_ref = refs[n_in:n_in + 3]
    cast_dst = refs[n_in + 3:n_in + 3 + n_cast]
    xa_ext, glu_ext, ycat, act, x1_buf, h2b = refs[n_in + 3 + n_cast:]
    T = TOKEN_TILE
    s = pl.program_id(0)
    tile_a = jnp.minimum(s, n_tiles - 1)
    n_a = tile_a // n_t
    t_a = tile_a - n_a * n_t
    n_b = jnp.maximum(s - 1, 0) // n_t
    row = lambda ref: ref[layer:layer + 1, :]
    mod = lambda k, n: mod_ref[k, pl.ds(n, 1), :]

    @pl.when(t_a == 0)
    def _():
        xa_ext[0:POOL_HIST, :] = jnp.zeros((POOL_HIST, D_POOL), _F32)
        glu_ext[0:CONV_HIST, :] = jnp.zeros((CONV_HIST, D_CONV), _F32)
        glu_ext[CONV_HIST + T:CONV_HIST + T + SUBLANES, :] = jnp.zeros((SUBLANES, D_CONV), _F32)

    def finish_mixer(x1):
        x1_buf[...] = x1
        h2 = _mod_norm(x1, row(ng2_ref), mod(4, n_a), mod(3, n_a))
        h2b[...] = h2.astype(_BF16)

    def run(order):
        stages = {
            "a": _mixer_prompt_stages(
                x_ref, mod(0, n_a), mod(1, n_a), mod(2, n_a), row(ng1_ref), t_a, win_ref, poolw_ref,
                row(pools_ref), convw_ref, row(convb_ref), row(clg_ref), row(clb_ref), row(glg_ref),
                row(glb_ref), wsp_ref, bsf_ref, wout_ref, xa_ext, glu_ext, ycat, finish_mixer),
            "b": _ffn_prompt_stages(o_ref, x1_buf, h2b, mod(5, n_b), w1_ref, w3_ref, w2_ref,
                                    fg_ref[...], act, final_norm),
        }
        for i, which in enumerate(order):
            next(stages[which])
            if i == _CAST_AFTER_STAGE:
                for src, dst in zip(cast_src, cast_dst):
                    dst[...] = src[...].astype(_BF16)
        for which in set(order):
            assert next(stages[which], "done") == "done"

    pl.when(s == 0)(lambda: run(_STAGE_ORDER.replace("b", "")))
    pl.when(jnp.logical_and(s > 0, s < n_tiles))(lambda: run(_STAGE_ORDER))
    pl.when(s == n_tiles)(lambda: run(_STAGE_ORDER.replace("a", "")))

    @pl.when(jnp.logical_and(t_a == n_t - 1, s < n_tiles))
    def _():
        npool_ref[...] = xa_ext[pl.ds(POOL_HIST + T - POOL_BUF, POOL_BUF), :]
        nconv_ref[...] = glu_ext[pl.ds(CONV_HIST + T - CONV_BUF, CONV_BUF), :]


def _resident(shape, index):
    return pl.BlockSpec(shape, lambda *_: index, pipeline_mode=pl.Buffered(1))


def _param_specs(layer, depth):
    per_layer_rows = _resident((depth, D_MODEL), (0, 0))
    return dict(
        ng=per_layer_rows,
        fg=_resident((1, D_MODEL), (0, 0)),
        w_in=_resident((D_MODEL, D_IN), (0, 0)),
        pool_bd=_resident((None, D_POOL, D_POOL), (layer, 0, 0)),
        rows256=_resident((depth, D_POOL), (0, 0)),
        conv_w=_resident((None, CONV_WIDTH, D_CONV), (layer, 0, 0)),
        rows512=_resident((depth, D_GMLP), (0, 0)),
        ws_pairs=_resident((None, N_GMLP_HEADS // 2, CHUNK, 2 * CHUNK), (layer, 0, 0, 0)),
        bs_full=_resident((None, CHUNK, D_GMLP), (layer, 0, 0)),
        w_out=_resident((D_MODEL, D_MODEL), (0, 0)),
        w_ff13=_resident((D_MODEL, D_FF), (0, 0)),
        w_ff2=_resident((D_FF, D_MODEL), (0, 0)),
    )


def _cast_rows(n_rows, n_steps):
    bf16_rows = 2 * SUBLANES
    for rows in range(bf16_rows, n_rows + 1, bf16_rows):
        if n_rows % rows == 0 and n_rows // rows <= n_steps:
            return rows
    raise ValueError((n_rows, n_steps))


def _layer_prompt_call(layer, x, mod, n_sample, P, W, next_f32, final_norm):
    nb, seq, _ = x.shape
    depth = P["norm1_g"].shape[0]
    T = TOKEN_TILE
    n_t = seq // T
    n_tiles = nb * n_t
    n_steps = n_tiles + 1
    sp = _param_specs(layer, depth)
    cast_in, cast_specs_in, cast_shapes, cast_specs_out = [], [], [], []
    if next_f32 is not None:
        for name in _BIG_WEIGHTS:
            w = next_f32[name]
            n_rows, n_cols = w.shape[1:]
            rows = _cast_rows(n_rows, n_steps)
            last = n_rows // rows - 1
            cast_in.append(w)
            cast_specs_in.append(pl.BlockSpec(
                (None, rows, n_cols), lambda s, last=last: (layer + 1, jnp.minimum(s, last), 0)))
            cast_shapes.append(jax.ShapeDtypeStruct((n_rows, n_cols), _BF16))
            cast_specs_out.append(pl.BlockSpec(
                (rows, n_cols), lambda s, last=last: (jnp.minimum(s, last), 0)))

    def x_map(s):
        tile = jnp.minimum(s, n_tiles - 1)
        return (tile // n_t, tile % n_t, 0)

    def o_map(s):
        tile = jnp.maximum(s - 1, 0)
        return (tile // n_t, tile % n_t, 0)

    def state_map(s):
        return (jnp.minimum(s, n_tiles - 1) // n_t, 0, 0)

    in_specs = [
        pl.BlockSpec((None, T, D_MODEL), x_map),
        pl.BlockSpec((None, N_MOD, nb, D_MODEL), lambda s: (layer, 0, n_sample // nb, 0),
                     pipeline_mode=pl.Buffered(1)),
        sp["ng"], sp["ng"], sp["fg"], sp["w_in"], sp["pool_bd"], sp["rows256"], sp["conv_w"],
        sp["rows256"], sp["rows256"], sp["rows256"], sp["rows512"], sp["rows512"], sp["ws_pairs"],
        sp["bs_full"], sp["w_out"], sp["w_ff13"], sp["w_ff13"], sp["w_ff2"],
    ]
    assert len(in_specs) == _N_PROMPT_INPUTS
    out_shape = [
        jax.ShapeDtypeStruct((nb, seq, D_MODEL), _F32),
        jax.ShapeDtypeStruct((nb, POOL_BUF, D_POOL), _F32),
        jax.ShapeDtypeStruct((nb, CONV_BUF, D_CONV), _F32),
    ]
    out_specs = [
        pl.BlockSpec((None, T, D_MODEL), o_map),
        pl.BlockSpec((None, POOL_BUF, D_POOL), state_map),
        pl.BlockSpec((None, CONV_BUF, D_CONV), state_map),
    ]
    outs = pl.pallas_call(
        functools.partial(_layer_prompt_kernel, layer=layer, n_t=n_t, n_tiles=n_tiles,
                          final_norm=final_norm, n_cast=len(cast_in)),
        out_shape=out_shape + cast_shapes,
        grid=(n_steps,),
        in_specs=in_specs + cast_specs_in,
        out_specs=out_specs + cast_specs_out,
        scratch_shapes=[
            pltpu.VMEM((T + POOL_HIST, D_POOL), _F32),
            pltpu.VMEM((T + CONV_HIST + SUBLANES, D_CONV), _F32),
            pltpu.VMEM((T, D_MODEL), _BF16),
            pltpu.VMEM((T, D_FF), _BF16),
            pltpu.VMEM((T, D_MODEL), _F32),
            pltpu.VMEM((T, D_MODEL), _BF16),
        ],
        compiler_params=pltpu.CompilerParams(
            dimension_semantics=("arbitrary",), vmem_limit_bytes=VMEM_LIMIT),
        name="layer_prompt",
    )(x, mod, P["norm1_g"], P["norm2_g"], P["final_g"], W["w_in"], P["pool_bd"], P["pool_scale"],
      P["conv_w"], P["conv_b"], P["conv_ln_g"], P["conv_ln_b"], P["gmlp_ln_g"], P["gmlp_ln_b"],
      P["ws_pairs"], P["bs_full"], W["w_out"], W["w_ff1"], W["w_ff3"], W["w_ff2"], *cast_in)
    next_w = dict(zip(_BIG_WEIGHTS, outs[3:])) if cast_in else None
    return outs[0], outs[1], outs[2], next_w


_N_SAMPLE_INPUTS = 22


def _layer_sample_kernel(*refs, layer, final_norm, cast):
    (x_ref, mod_ref, ng1_ref, ng2_ref, fg_ref, win_ref, poolw_ref, pools_ref, convw_ref, convb_ref,
     clg_ref, clb_ref, glg_ref, glb_ref, ws0_ref, bs0_ref, wout_ref, w1_ref, w3_ref, w2_ref,
     spool_ref, sconv_ref) = refs[:_N_SAMPLE_INPUTS]
    o_ref, npool_ref, nconv_ref, vrow_ref = refs[_N_SAMPLE_INPUTS:_N_SAMPLE_INPUTS + 4]
    copies = dict(zip(_BIG_WEIGHTS, refs[_N_SAMPLE_INPUTS + 4:-3])) if cast else {}
    x1_buf, h2b, acc_ref = refs[-3:]
    step = pl.program_id(0)
    row = lambda ref: ref[layer:layer + 1, :]

    def weight(name, ref, cols=slice(None)):
        w = ref[:, cols]
        if cast:
            w = w.astype(_BF16)
            copies[name][:, cols] = w
        return w

    @pl.when(step == 0)
    def _():
        x = x_ref[...]
        h = _mod_norm(x, row(ng1_ref), mod_ref[1], mod_ref[0])
        hb = h.astype(_BF16)
        lo = _lo_lanes()
        win = lambda c0, c1: weight("w_in", win_ref, slice(c0, c1))

        xa = _dot(hb, win(0, D_POOL))
        means = []
        for half in range(2):
            cols = slice(half * LANES, (half + 1) * LANES)
            w_a, w_b = POOL_WINDOWS[2 * half], POOL_WINDOWS[2 * half + 1]
            s = xa[:, cols]
            for j in range(1, w_a):
                s = s + spool_ref[POOL_BUF - j, :, cols]
            s_a = s
            for j in range(w_a, w_b):
                s = s + spool_ref[POOL_BUF - j, :, cols]
            means.append(jnp.where(lo, s_a, s) / jnp.where(lo, float(w_a), float(w_b)))
        ya = _pool_mix(means, xa, poolw_ref, row(pools_ref))
        npool_ref[0:POOL_BUF - 1] = spool_ref[1:POOL_BUF]
        npool_ref[POOL_BUF - 1] = xa

        a = _dot(hb, win(D_POOL, D_POOL + D_CONV))
        gt = _dot(hb, win(D_POOL + D_CONV, D_POOL + 2 * D_CONV))
        glu = a * _sigmoid(gt)
        acc = row(convb_ref) + convw_ref[CONV_BUF:CONV_WIDTH, :] * glu
        for r in range(CONV_BUF):
            acc = acc + convw_ref[r:r + 1, :] * sconv_ref[r]
        yb = _silu(_layer_norm(acc, row(clg_ref), row(clb_ref)))
        nconv_ref[0:CONV_BUF - 1] = sconv_ref[1:CONV_BUF]
        nconv_ref[CONV_BUF - 1] = glu

        off_u = D_POOL + 2 * D_CONV
        u = _gelu_tanh(_dot(hb, win(off_u, off_u + D_GMLP)))
        v = _gelu_tanh(_dot(hb, win(off_u + D_GMLP, off_u + 2 * D_GMLP)))
        v = _layer_norm(v, row(glg_ref), row(glb_ref))
        vrow_ref[...] = v
        yc = u * (row(ws0_ref) * v + row(bs0_ref))

        ycat = jnp.concatenate([ya, yb, yc], axis=-1).astype(_BF16)
        x1 = x + mod_ref[2] * _dot(ycat, weight("w_out", wout_ref))
        x1_buf[...] = x1
        h2b[...] = _mod_norm(x1, row(ng2_ref), mod_ref[4], mod_ref[3]).astype(_BF16)
        acc_ref[...] = jnp.zeros(acc_ref.shape, _F32)

    @pl.when(step > 0)
    def _():
        a1 = _dot(h2b[...], weight("w_ff1", w1_ref))
        a3 = _dot(h2b[...], weight("w_ff3", w3_ref))
        act = (_silu(a1) * a3).astype(_BF16)
        acc_ref[...] += _dot(act, weight("w_ff2", w2_ref))

    @pl.when(step == pl.num_programs(0) - 1)
    def _():
        out = x1_buf[...] + mod_ref[5] * acc_ref[...]
        if final_norm:
            out = _rms_norm(out, fg_ref[...])
        o_ref[...] = out


def _layer_sample_call(layer, xs, mod, P, W, state_pool, state_conv, final_norm, cast):
    n = xs.shape[0]
    depth = P["norm1_g"].shape[0]
    sp = _param_specs(layer, depth)
    n_chunks = D_FF // FF_CHUNK
    chunk = lambda s: jnp.maximum(s - 1, 0)

    def w_spec(shape, index, **kw):
        if cast:
            return pl.BlockSpec((None,) + shape, lambda s: (layer,) + index(s), **kw)
        return pl.BlockSpec(shape, index, **kw)

    once = dict(pipeline_mode=pl.Buffered(1))
    w_specs = dict(
        w_in=w_spec((D_MODEL, D_IN), lambda s: (0, 0), **once),
        w_out=w_spec((D_MODEL, D_MODEL), lambda s: (0, 0), **once),
        w_ff1=w_spec((D_MODEL, FF_CHUNK), lambda s: (0, chunk(s))),
        w_ff3=w_spec((D_MODEL, FF_CHUNK), lambda s: (0, chunk(s))),
        w_ff2=w_spec((FF_CHUNK, D_MODEL), lambda s: (chunk(s), 0)),
    )
    in_specs = [
        _resident((n, D_MODEL), (0, 0)),
        _resident((None, N_MOD, n, D_MODEL), (layer, 0, 0, 0)),
        sp["ng"], sp["ng"], sp["fg"], w_specs["w_in"], sp["pool_bd"], sp["rows256"], sp["conv_w"],
        sp["rows256"], sp["rows256"], sp["rows256"], sp["rows512"], sp["rows512"], sp["rows512"],
        sp["rows512"], w_specs["w_out"], w_specs["w_ff1"], w_specs["w_ff3"], w_specs["w_ff2"],
        _resident((None, POOL_BUF, n, D_POOL), (layer, 0, 0, 0)),
        _resident((None, CONV_BUF, n, D_CONV), (layer, 0, 0, 0)),
    ]
    assert len(in_specs) == _N_SAMPLE_INPUTS
    out_shape = [
        jax.ShapeDtypeStruct((n, D_MODEL), _F32),
        jax.ShapeDtypeStruct((POOL_BUF, n, D_POOL), _F32),
        jax.ShapeDtypeStruct((CONV_BUF, n, D_CONV), _F32),
        jax.ShapeDtypeStruct((n, D_GMLP), _F32),
    ]
    whole = lambda shape: pl.BlockSpec(shape, lambda s: (0,) * len(shape))
    out_specs = [whole((n, D_MODEL)), whole((POOL_BUF, n, D_POOL)), whole((CONV_BUF, n, D_CONV)),
                 whole((n, D_GMLP))]
    if cast:
        out_shape += [jax.ShapeDtypeStruct(W[name].shape[1:], _BF16) for name in _BIG_WEIGHTS]
        out_specs += [
            whole((D_MODEL, D_IN)), whole((D_MODEL, D_MODEL)),
            pl.BlockSpec((D_MODEL, FF_CHUNK), lambda s: (0, chunk(s))),
            pl.BlockSpec((D_MODEL, FF_CHUNK), lambda s: (0, chunk(s))),
            pl.BlockSpec((FF_CHUNK, D_MODEL), lambda s: (chunk(s), 0)),
        ]
    outs = pl.pallas_call(
        functools.partial(_layer_sample_kernel, layer=layer, final_norm=final_norm, cast=cast),
        out_shape=out_shape,
        grid=(n_chunks + 1,),
        in_specs=in_specs,
        out_specs=out_specs,
        scratch_shapes=[pltpu.VMEM((n, D_MODEL), _F32), pltpu.VMEM((n, D_MODEL), _BF16),
                        pltpu.VMEM((n, D_MODEL), _F32)],
        compiler_params=pltpu.CompilerParams(
            dimension_semantics=("arbitrary",), vmem_limit_bytes=VMEM_LIMIT),
        name="layer_sample",
    )(xs, mod, P["norm1_g"], P["norm2_g"], P["final_g"], W["w_in"], P["pool_bd"], P["pool_scale"],
      P["conv_w"], P["conv_b"], P["conv_ln_g"], P["conv_ln_b"], P["gmlp_ln_g"], P["gmlp_ln_b"],
      P["ws0"], P["bs0"], W["w_out"], W["w_ff1"], W["w_ff3"], W["w_ff2"], state_pool, state_conv)
    bf16_w = dict(zip(_BIG_WEIGHTS, outs[4:])) if cast else None
    return outs[0], outs[1], outs[2], outs[3], bf16_w


def _prep_params(norm1_g, norm2_g, pool_w, pool_scale, conv_w, conv_b, conv_ln_g, conv_ln_b,
                 gmlp_ln_g, gmlp_ln_b, gmlp_ws, gmlp_bs, final_g):
    depth = pool_w.shape[0]
    n_grp = len(POOL_WINDOWS)
    eye = jnp.eye(n_grp, dtype=_F32)
    pool_bd = (eye[None, :, None, :, None] * pool_w[:, :, :, None, :]).reshape(depth, D_POOL, D_POOL)
    mask = jnp.tril(jnp.ones((CHUNK, CHUNK), dtype=bool))
    ws = jnp.where(mask, gmlp_ws, jnp.zeros_like(gmlp_ws))
    ws_pairs = ws.reshape(depth, N_GMLP_HEADS // 2, 2, CHUNK, CHUNK).transpose(0, 1, 3, 2, 4).reshape(
        depth, N_GMLP_HEADS // 2, CHUNK, 2 * CHUNK)
    bs_full = jnp.repeat(gmlp_bs.transpose(0, 2, 1), GMLP_HEAD, axis=2)
    ws0 = jnp.repeat(gmlp_ws[:, :, 0, 0], GMLP_HEAD, axis=1)
    bs0 = jnp.repeat(gmlp_bs[:, :, 0], GMLP_HEAD, axis=1)
    return dict(
        norm1_g=norm1_g, norm2_g=norm2_g, final_g=final_g.reshape(1, D_MODEL),
        pool_bd=pool_bd.astype(_BF16), pool_scale=pool_scale,
        conv_w=conv_w, conv_b=conv_b, conv_ln_g=conv_ln_g, conv_ln_b=conv_ln_b,
        gmlp_ln_g=gmlp_ln_g, gmlp_ln_b=gmlp_ln_b,
        ws_pairs=ws_pairs.astype(_BF16), bs_full=bs_full, ws0=ws0, bs0=bs0,
    )


def kernel(x_prompt, x_sample, c_prompt, c_sample, state_pool, state_conv, w_ada, b_ada, norm1_g, norm2_g, w_in, pool_w, pool_scale, conv_w, conv_b, conv_ln_g, conv_ln_b, gmlp_ln_g, gmlp_ln_b, gmlp_ws, gmlp_bs, w_out, w_ff1, w_ff3, w_ff2, final_g):
    depth = w_in.shape[0]
    nb = x_prompt.shape[0]
    ns = x_sample.shape[0]
    assert x_sample.shape[1] == 1 and x_prompt.shape[1] % TOKEN_TILE == 0 and ns % nb == 0

    c_all = jnp.concatenate([c_sample, c_prompt], axis=0)
    mod = _ada_call(c_all, w_ada, b_ada)
    P = _prep_params(norm1_g, norm2_g, pool_w, pool_scale, conv_w, conv_b, conv_ln_g, conv_ln_b,
                     gmlp_ln_g, gmlp_ln_b, gmlp_ws, gmlp_bs, final_g)
    big_f32 = dict(w_in=w_in, w_out=w_out, w_ff1=w_ff1, w_ff3=w_ff3, w_ff2=w_ff2)

    xp = x_prompt
    xs = x_sample.reshape(ns, D_MODEL)
    spool_t = state_pool.transpose(0, 2, 1, 3)
    sconv_t = state_conv.transpose(0, 2, 1, 3)
    pool_p, conv_p, pool_s, conv_s, v_s = [], [], [], [], []
    W = None
    for l in range(depth):
        last = l == depth - 1
        xs, npool_s, nconv_s, nv_s, cast_w = _layer_sample_call(
            l, xs, mod, P, big_f32 if W is None else W, spool_t, sconv_t, last, cast=W is None)
        W = cast_w if W is None else W
        xp, npool_p, nconv_p, W = _layer_prompt_call(
            l, xp, mod, ns, P, W, None if last else big_f32, last)
        pool_p.append(npool_p)
        conv_p.append(nconv_p)
        pool_s.append(npool_s)
        conv_s.append(nconv_s)
        v_s.append(nv_s.reshape(ns, 1, D_GMLP))
    return (xp, xs.reshape(ns, 1, D_MODEL), jnp.stack(pool_p), jnp.stack(conv_p),
            jnp.stack(pool_s).transpose(0, 2, 1, 3), jnp.stack(conv_s).transpose(0, 2, 1, 3),
            jnp.stack(v_s))
```

```python
import functools

import jax
import jax.numpy as jnp
from jax.experimental import pallas as pl
from jax.experimental.pallas import tpu as pltpu

D_MODEL = 1024
D_POOL = 256
POOL_WINDOWS = (2, 4, 8, 16)
POOL_GROUP = 64
POOL_BUF = 15
D_CONV = 256
CONV_WIDTH = 31
CONV_BUF = 30
D_GMLP = 512
GMLP_HEAD = 64
N_GMLP_HEADS = 8
CHUNK = 128
D_IN = D_POOL + 2 * D_CONV + 2 * D_GMLP
D_FF = 2816
N_MOD = 6
EPS = 1e-6

LANES = 128
SUBLANES = 8

TOKEN_TILE = 512
FF_CHUNK = 256
ADA_TERMS = 2
POOL_HIST = 16
CONV_HIST = 32
VMEM_LIMIT = 60 * 1024 * 1024

_BF16 = jnp.bfloat16
_F32 = jnp.float32


def _dot(a, b):
    return jnp.dot(a, b, preferred_element_type=_F32)


_LOG2E = 1.4426950408889634


def _sigmoid(x):
    return 1.0 / (1.0 + jnp.exp2(x * -_LOG2E))


def _silu(x):
    return x * _sigmoid(x)


def _gelu_tanh(x):
    k1 = -2.0 * 0.7978845608028654 * _LOG2E
    k3 = k1 * 0.044715
    return x / (1.0 + jnp.exp2(x * (k1 + k3 * (x * x))))


def _rms_norm(x, g):
    ms = jnp.mean(x * x, axis=-1, keepdims=True)
    return (x * jax.lax.rsqrt(ms + EPS)) * g


def _mod_norm(x, g, sc, sh):
    ms = jnp.mean(x * x, axis=-1, keepdims=True)
    return (x * jax.lax.rsqrt(ms + EPS)) * (g * (1.0 + sc)) + sh


def _layer_norm(x, g, b):
    mu = jnp.mean(x, axis=-1, keepdims=True)
    xc = x - mu
    var = jnp.mean(xc * xc, axis=-1, keepdims=True)
    return xc * jax.lax.rsqrt(var + EPS) * g + b


def _lo_lanes():
    return jax.lax.broadcasted_iota(jnp.int32, (1, LANES), 1) < POOL_GROUP


def _ada_kernel(c_ref, w_ref, b_ref, o_ref):
    l = pl.program_id(0)
    s = _silu(c_ref[...]).astype(_BF16)
    res = _dot(s, w_ref[...].astype(_BF16)) + b_ref[pl.ds(l, 1), :]
    for k in range(ADA_TERMS):
        o_ref[k] = res[:, k * D_MODEL:(k + 1) * D_MODEL]


def _ada_call(c_all, w_ada, b_ada):
    depth = w_ada.shape[0]
    n = c_all.shape[0]
    cols = ADA_TERMS * D_MODEL
    return pl.pallas_call(
        _ada_kernel,
        out_shape=jax.ShapeDtypeStruct((depth, N_MOD, n, D_MODEL), _F32),
        grid=(depth, N_MOD // ADA_TERMS),
        in_specs=[
            pl.BlockSpec((n, D_MODEL), lambda l, j: (0, 0)),
            pl.BlockSpec((None, D_MODEL, cols), lambda l, j: (l, 0, j)),
            pl.BlockSpec((depth, cols), lambda l, j: (0, j)),
        ],
        out_specs=pl.BlockSpec((None, ADA_TERMS, n, D_MODEL), lambda l, j: (l, j, 0, 0)),
        compiler_params=pltpu.CompilerParams(
            dimension_semantics=("arbitrary", "arbitrary"), vmem_limit_bytes=VMEM_LIMIT),
        name="ada_mod",
    )(c_all, w_ada, b_ada)


def _pool_mix(means, xa, poolw_ref, pools):
    d = jnp.concatenate(means, axis=-1) - xa
    return _dot(d.astype(_BF16), poolw_ref[...]) * pools


def _mixer_prompt_stages(x_ref, sh, sc, g, ng, t, win_ref, poolw_ref, pools, convw_ref, convb, clg, clb,
                         glg, glb, wsp_ref, bsf_ref, wout_ref, xa_ext, glu_ext, ycat, finish):
    T = TOKEN_TILE
    x = x_ref[...]
    h = _mod_norm(x, ng, sc, sh)
    hb = h.astype(_BF16)
    lo = _lo_lanes()
    yield

    xa = _dot(hb, win_ref[:, 0:D_POOL])
    a = _dot(hb, win_ref[:, D_POOL:D_POOL + D_CONV])
    gt = _dot(hb, win_ref[:, D_POOL + D_CONV:D_POOL + 2 * D_CONV])
    yield
    xa_ext[POOL_HIST:POOL_HIST + T, :] = xa
    glu_ext[CONV_HIST:CONV_HIST + T, :] = a * _sigmoid(gt)
    pos1 =(t * T + 1 + jax.lax.broadcasted_iota(jnp.int32, (T, LANES), 0)).astype(_F32)
    means = []
    for half in range(2):
        w_a, w_b = POOL_WINDOWS[2 * half], POOL_WINDOWS[2 * half + 1]
        cur = xa_ext[:, half * LANES:(half + 1) * LANES]
        sums = {1: cur}
        w = 1
        while w < w_b:
            cur = cur + pltpu.roll(cur, w, axis=0)
            w *= 2
            sums[w] = cur
        s_a = sums[w_a][POOL_HIST:POOL_HIST + T, :]
        s_b = sums[w_b][POOL_HIST:POOL_HIST + T, :]
        cnt = jnp.minimum(pos1, jnp.where(lo, float(w_a), float(w_b)))
        means.append(jnp.where(lo, s_a, s_b) / cnt)
    ya = _pool_mix(means, xa, poolw_ref, pools)
    ycat[:, 0:D_POOL] = ya.astype(_BF16)
    yield

    off_u = D_POOL + 2 * D_CONV
    u = _gelu_tanh(_dot(hb, win_ref[:, off_u:off_u + D_GMLP]))
    v = _gelu_tanh(_dot(hb, win_ref[:, off_u + D_GMLP:off_u + 2 * D_GMLP]))
    v = _layer_norm(v, glg, glb)
    yield

    lead = CONV_HIST - CONV_BUF
    acc = convb
    for r in range(SUBLANES):
        part = None
        for q in range((lead + CONV_WIDTH - 1) // SUBLANES + 1):
            j = SUBLANES * q + r
            if lead <= j < lead + CONV_WIDTH:
                term = convw_ref[j - lead:j - lead + 1, :] * glu_ext[SUBLANES * q:SUBLANES * q + T + SUBLANES, :]
                part = term if part is None else part + term
        acc = acc + part[r:r + T, :]
    yb = _silu(_layer_norm(acc, clg, clb))
    ycat[:, D_POOL:D_POOL + D_CONV] = yb.astype(_BF16)
    yield

    zero = jnp.zeros((CHUNK, LANES), _F32)
    for cp in range(T // (2 * CHUNK)):
        r0 = cp * 2 * CHUNK
        for m in range(N_GMLP_HEADS // 2):
            l0 = m * LANES
            v0 = v[r0:r0 + CHUNK, l0:l0 + LANES]
            v1 = v[r0 + CHUNK:r0 + 2 * CHUNK, l0:l0 + LANES]
            top = jnp.concatenate([jnp.where(lo, v0, zero), jnp.where(lo, v1, zero)], axis=1)
            bot = jnp.concatenate([jnp.where(lo, zero, v0), jnp.where(lo, zero, v1)], axis=1)
            rhs = jnp.concatenate([top, bot], axis=0).astype(_BF16)
            z = _dot(wsp_ref[m], rhs)
            bias = bsf_ref[:, l0:l0 + LANES]
            y0 = u[r0:r0 + CHUNK, l0:l0 + LANES] * (z[:, 0:LANES] + bias)
            y1 = u[r0 + CHUNK:r0 + 2 * CHUNK, l0:l0 + LANES] * (z[:, LANES:2 * LANES] + bias)
            c0 = D_POOL + D_CONV + l0
            ycat[r0:r0 + CHUNK, c0:c0 + LANES] = y0.astype(_BF16)
            ycat[r0 + CHUNK:r0 + 2 * CHUNK, c0:c0 + LANES] = y1.astype(_BF16)
    yield

    y_lo = _dot(ycat[...], wout_ref[:, 0:D_MODEL // 2])
    yield
    y_hi = _dot(ycat[...], wout_ref[:, D_MODEL // 2:D_MODEL])
    finish(x + g * jnp.concatenate([y_lo, y_hi], axis=-1))
    xa_ext[0:POOL_HIST, :] = xa_ext[T:T + POOL_HIST, :]
    glu_ext[0:CONV_HIST, :] = glu_ext[T:T + CONV_HIST, :]
    yield


def _ffn_prompt_stages(o_ref, x1_buf, h2b, g, w1_ref, w3_ref, w2_ref, fg, act, final_norm):
    o_ref[...] = x1_buf[...]
    yield
    for c in range(D_FF // FF_CHUNK):
        cols = slice(c * FF_CHUNK, (c + 1) * FF_CHUNK)
        a1 = _dot(h2b[...], w1_ref[:, cols])
        a3 = _dot(h2b[...], w3_ref[:, cols])
        act[:, cols] = (_silu(a1) * a3).astype(_BF16)
        yield
    half = D_MODEL // 2
    lo = slice(0, half)
    o_ref[:, lo] = o_ref[:, lo] + g[:, lo] * _dot(act[...], w2_ref[:, lo])
    yield
    hi = slice(half, D_MODEL)
    out_hi = o_ref[:, hi] + g[:, hi] * _dot(act[...], w2_ref[:, hi])
    if final_norm:
        o_ref[...] = _rms_norm(jnp.concatenate([o_ref[:, lo], out_hi], axis=-1), fg)
    else:
        o_ref[:, hi] = out_hi
    yield


_STAGE_ORDER = "ba" + "bbba" + "bba" + "bba" + "bb" + "aa" + "bba" + "ba" + "b"
_CAST_AFTER_STAGE = 3
_N_PROMPT_INPUTS = 20
_BIG_WEIGHTS = ("w_in", "w_out", "w_ff1", "w_ff3", "w_ff2")


def _layer_prompt_kernel(*refs, layer, n_t, n_tiles, final_norm, n_cast, n_stack):
    n_in = _N_PROMPT_INPUTS + n_cast + 2 * n_stack
    n_out = 3 + n_cast + (2 if n_stack else 0)
    (x_ref, mod_ref, ng1_ref, ng2_ref, fg_ref, win_ref, poolw_ref, pools_ref, convw_ref, convb_ref,
     clg_ref, clb_ref, glg_ref, glb_ref, wsp_ref, bsf_ref, wout_ref, w1_hbm, w3_hbm,
     w2_hbm) = refs[:_N_PROMPT_INPUTS]
    cast_src = refs[_N_PROMPT_INPUTS:_N_PROMPT_INPUTS + n_cast]
    stack_src = refs[_N_PROMPT_INPUTS + n_cast:n_in]
    o_ref, npool_ref, nconv_ref = refs[n_in:n_in + 3]
    cast_dst = refs[n_in + 3:n_in + 3 + n_cast]
    stack_dst = refs[n_in + 3 + n_cast:n_in + n_out]
    (xa_ext, glu_ext, ycat, act, x1_buf, h2b, w1_ref, w3_ref, w2_ref,
     ffn_sem, *stack_sem) = refs[n_in + n_out:]
    T = TOKEN_TILE
    s = pl.program_id(0)
    tile_a = jnp.minimum(s, n_tiles - 1)
    n_a = tile_a // n_t
    t_a = tile_a - n_a * n_t
    n_b = jnp.maximum(s - 1, 0) // n_t
    row = lambda ref: ref[layer:layer + 1, :]
    mod = lambda k, n: mod_ref[k, pl.ds(n, 1), :]

    @pl.when(t_a == 0)
    def _():
        xa_ext[0:POOL_HIST, :] = jnp.zeros((POOL_HIST, D_POOL), _F32)
        glu_ext[0:CONV_HIST, :] = jnp.zeros((CONV_HIST, D_CONV), _F32)
        glu_ext[CONV_HIST + T:CONV_HIST + T + SUBLANES, :] = jnp.zeros((SUBLANES, D_CONV), _F32)

    def finish_mixer(x1):
        x1_buf[...] = x1
        h2 = _mod_norm(x1, row(ng2_ref), mod(4, n_a), mod(3, n_a))
        h2b[...] = h2.astype(_BF16)

    def run(order):
        stages = {
            "a": _mixer_prompt_stages(
                x_ref, mod(0, n_a), mod(1, n_a), mod(2, n_a), row(ng1_ref), t_a, win_ref, poolw_ref,
                row(pools_ref), convw_ref, row(convb_ref), row(clg_ref), row(clb_ref), row(glg_ref),
                row(glb_ref), wsp_ref, bsf_ref, wout_ref, xa_ext, glu_ext, ycat, finish_mixer),
            "b": _ffn_prompt_stages(o_ref, x1_buf, h2b, mod(5, n_b), w1_ref, w3_ref, w2_ref,
                                    fg_ref[...], act, final_norm),
        }
        for i, which in enumerate(order):
            next(stages[which])
            if i == _CAST_AFTER_STAGE:
                for src, dst in zip(cast_src, cast_dst):
                    dst[...] = src[...].astype(_BF16)
        for which in set(order):
            assert next(stages[which], "done") == "done"

    ffn_copies = [pltpu.make_async_copy(src, dst, ffn_sem.at[i]) for i, (src, dst) in enumerate(
        ((w1_hbm, w1_ref), (w3_hbm, w3_ref), (w2_hbm, w2_ref)))]
    stack_copies = [
        pltpu.make_async_copy(src, stack_dst[i // n_stack].at[i % n_stack], stack_sem[0].at[i])
        for i, src in enumerate(stack_src)]

    @pl.when(s == 0)
    def _():
        for copy in ffn_copies + stack_copies:
            copy.start()

    @pl.when(s == 1)
    def _():
        for copy in ffn_copies:
            copy.wait()

    pl.when(s == 0)(lambda: run(_STAGE_ORDER.replace("b", "")))
    pl.when(jnp.logical_and(s > 0, s < n_tiles))(lambda: run(_STAGE_ORDER))
    pl.when(s == n_tiles)(lambda: run(_STAGE_ORDER.replace("a", "")))

    @pl.when(jnp.logical_and(t_a == n_t - 1, s < n_tiles))
    def _():
        npool_ref[...] = xa_ext[pl.ds(POOL_HIST + T - POOL_BUF, POOL_BUF), :]
        nconv_ref[...] = glu_ext[pl.ds(CONV_HIST + T - CONV_BUF, CONV_BUF), :]

    @pl.when(s == n_tiles)
    def _():
        for copy in stack_copies:
            copy.wait()


def _resident(shape, index):
    return pl.BlockSpec(shape, lambda *_: index, pipeline_mode=pl.Buffered(1))


def _param_specs(layer, depth):
    per_layer_rows = _resident((depth, D_MODEL), (0, 0))
    return dict(
        ng=per_layer_rows,
        fg=_resident((1, D_MODEL), (0, 0)),
        w_in=_resident((D_MODEL, D_IN), (0, 0)),
        pool_bd=_resident((None, D_POOL, D_POOL), (layer, 0, 0)),
        rows256=_resident((depth, D_POOL), (0, 0)),
        conv_w=_resident((None, CONV_WIDTH, D_CONV), (layer, 0, 0)),
        rows512=_resident((depth, D_GMLP), (0, 0)),
        ws_pairs=_resident((None, N_GMLP_HEADS // 2, CHUNK, 2 * CHUNK), (layer, 0, 0, 0)),
        bs_full=_resident((None, CHUNK, D_GMLP), (layer, 0, 0)),
        w_out=_resident((D_MODEL, D_MODEL), (0, 0)),
    )


def _cast_rows(n_rows, n_steps):
    bf16_rows = 2 * SUBLANES
    for rows in range(bf16_rows, n_rows + 1, bf16_rows):
        if n_rows % rows == 0 and n_rows // rows <= n_steps:
            return rows
    raise ValueError((n_rows, n_steps))


def _layer_prompt_call(layer, x, mod, n_sample, P, W, next_f32, final_norm, stack_states):
    nb, seq, _ = x.shape
    depth = P["norm1_g"].shape[0]
    T = TOKEN_TILE
    n_t = seq // T
    n_tiles = nb * n_t
    n_steps = n_tiles + 1
    sp = _param_specs(layer, depth)
    cast_in, cast_specs_in, cast_shapes, cast_specs_out = [], [], [], []
    if next_f32 is not None:
        for name in _BIG_WEIGHTS:
            w = next_f32[name]
            n_rows, n_cols = w.shape[1:]
            rows = _cast_rows(n_rows, n_steps)
            last = n_rows // rows - 1
            cast_in.append(w)
            cast_specs_in.append(pl.BlockSpec(
                (None, rows, n_cols), lambda s, last=last: (layer + 1, jnp.minimum(s, last), 0)))
            cast_shapes.append(jax.ShapeDtypeStruct((n_rows, n_cols), _BF16))
            cast_specs_out.append(pl.BlockSpec(
                (rows, n_cols), lambda s, last=last: (jnp.minimum(s, last), 0)))
    stack_in, stack_shapes, stack_scratch = (), [], []
    if stack_states is not None:
        n_stack = len(stack_states[0])
        assert len(stack_states[1]) == n_stack
        stack_in = tuple(stack_states[0]) + tuple(stack_states[1])
        stack_shapes = [jax.ShapeDtypeStruct((n_stack,) + group[0].shape, _F32) for group in stack_states]
        stack_scratch = [pltpu.SemaphoreType.DMA((len(stack_in),))]
    in_hbm = pl.BlockSpec(memory_space=pl.ANY)

    def x_map(s):
        tile = jnp.minimum(s, n_tiles - 1)
        return (tile // n_t, tile % n_t, 0)

    def o_map(s):
        tile = jnp.maximum(s - 1, 0)
        return (tile // n_t, tile % n_t, 0)

    def state_map(s):
        return (jnp.minimum(s, n_tiles - 1) // n_t, 0, 0)

    in_specs = [
        pl.BlockSpec((None, T, D_MODEL), x_map),
        pl.BlockSpec((None, N_MOD, nb, D_MODEL), lambda s: (layer, 0, n_sample // nb, 0),
                     pipeline_mode=pl.Buffered(1)),
        sp["ng"], sp["ng"], sp["fg"], sp["w_in"], sp["pool_bd"], sp["rows256"], sp["conv_w"],
        sp["rows256"], sp["rows256"], sp["rows256"], sp["rows512"], sp["rows512"], sp["ws_pairs"],
        sp["bs_full"], sp["w_out"],
        pl.BlockSpec(memory_space=pl.ANY), pl.BlockSpec(memory_space=pl.ANY),
        pl.BlockSpec(memory_space=pl.ANY),
    ]
    assert len(in_specs) == _N_PROMPT_INPUTS
    out_shape = [
        jax.ShapeDtypeStruct((nb, seq, D_MODEL), _F32),
        jax.ShapeDtypeStruct((nb, POOL_BUF, D_POOL), _F32),
        jax.ShapeDtypeStruct((nb, CONV_BUF, D_CONV), _F32),
    ]
    out_specs = [
        pl.BlockSpec((None, T, D_MODEL), o_map),
        pl.BlockSpec((None, POOL_BUF, D_POOL), state_map),
        pl.BlockSpec((None, CONV_BUF, D_CONV), state_map),
    ]
    outs = pl.pallas_call(
        functools.partial(_layer_prompt_kernel, layer=layer, n_t=n_t, n_tiles=n_tiles,
                          final_norm=final_norm, n_cast=len(cast_in), n_stack=len(stack_in) // 2),
        out_shape=out_shape + cast_shapes + stack_shapes,
        grid=(n_steps,),
        in_specs=in_specs + cast_specs_in + [in_hbm] * len(stack_in),
        out_specs=out_specs + cast_specs_out + [in_hbm] * len(stack_shapes),
        scratch_shapes=[
            pltpu.VMEM((T + POOL_HIST, D_POOL), _F32),
            pltpu.VMEM((T + CONV_HIST + SUBLANES, D_CONV), _F32),
            pltpu.VMEM((T, D_MODEL), _BF16),
            pltpu.VMEM((T, D_FF), _BF16),
            pltpu.VMEM((T, D_MODEL), _F32),
            pltpu.VMEM((T, D_MODEL), _BF16),
            pltpu.VMEM((D_MODEL, D_FF), _BF16),
            pltpu.VMEM((D_MODEL, D_FF), _BF16),
            pltpu.VMEM((D_FF, D_MODEL), _BF16),
            pltpu.SemaphoreType.DMA((3,)),
        ] + stack_scratch,
        compiler_params=pltpu.CompilerParams(
            dimension_semantics=("arbitrary",), vmem_limit_bytes=VMEM_LIMIT),
        name="layer_prompt",
    )(x, mod, P["norm1_g"], P["norm2_g"], P["final_g"], W["w_in"], P["pool_bd"], P["pool_scale"],
      P["conv_w"], P["conv_b"], P["conv_ln_g"], P["conv_ln_b"], P["gmlp_ln_g"], P["gmlp_ln_b"],
      P["ws_pairs"], P["bs_full"], W["w_out"], W["w_ff1"], W["w_ff3"], W["w_ff2"], *cast_in,
      *stack_in)
    next_w = dict(zip(_BIG_WEIGHTS, outs[3:])) if cast_in else None
    stacks = tuple(outs[3 + len(cast_in):]) if stack_in else None
    return outs[0], outs[1], outs[2], next_w, stacks


_N_SAMPLE_INPUTS = 22


def _layer_sample_kernel(*refs, layer, final_norm, cast):
    (x_ref, mod_ref, ng1_ref, ng2_ref, fg_ref, win_ref, poolw_ref, pools_ref, convw_ref, convb_ref,
     clg_ref, clb_ref, glg_ref, glb_ref, ws0_ref, bs0_ref, wout_ref, w1_ref, w3_ref, w2_ref,
     spool_ref, sconv_ref) = refs[:_N_SAMPLE_INPUTS]
    o_ref, npool_ref, nconv_ref, vrow_ref = refs[_N_SAMPLE_INPUTS:_N_SAMPLE_INPUTS + 4]
    step = pl.program_id(0)
    row = lambda ref: ref[layer:layer + 1, :]
    if cast:
        copies = dict(zip(_BIG_WEIGHTS, refs[_N_SAMPLE_INPUTS + 4:-3]))
        x1_buf, h2b, acc_ref = refs[-3:]
    else:
        x1_buf, h2b, acc_ref, w1_vmem, w3_vmem, w2_vmem, ffn_sem = refs[-7:]
        ffn_copies = [pltpu.make_async_copy(src, dst, ffn_sem.at[i]) for i, (src, dst) in enumerate(
            ((w1_ref, w1_vmem), (w3_ref, w3_vmem), (w2_ref, w2_vmem)))]
        w1_ref, w3_ref, w2_ref = w1_vmem, w3_vmem, w2_vmem

        @pl.when(step == 0)
        def _():
            for copy in ffn_copies:
                copy.start()

        @pl.when(step == 1)
        def _():
            for copy in ffn_copies:
                copy.wait()

    def weight(name, ref, cols=slice(None)):
        w = ref[:, cols]
        if cast:
            w = w.astype(_BF16)
            copies[name][:, cols] = w
        return w

    @pl.when(step == 0)
    def _():
        x = x_ref[...]
        h = _mod_norm(x, row(ng1_ref), mod_ref[1], mod_ref[0])
        hb = h.astype(_BF16)
        lo = _lo_lanes()
        win = lambda c0, c1: weight("w_in", win_ref, slice(c0, c1))

        xa = _dot(hb, win(0, D_POOL))
        means = []
        for half in range(2):
            cols = slice(half * LANES, (half + 1) * LANES)
            w_a, w_b = POOL_WINDOWS[2 * half], POOL_WINDOWS[2 * half + 1]
            s = xa[:, cols]
            for j in range(1, w_a):
                s = s + spool_ref[POOL_BUF - j, :, cols]
            s_a = s
            for j in range(w_a, w_b):
                s = s + spool_ref[POOL_BUF - j, :, cols]
            means.append(jnp.where(lo, s_a, s) / jnp.where(lo, float(w_a), float(w_b)))
        ya = _pool_mix(means, xa, poolw_ref, row(pools_ref))
        npool_ref[0:POOL_BUF - 1] = spool_ref[1:POOL_BUF]
        npool_ref[POOL_BUF - 1] = xa

        a = _dot(hb, win(D_POOL, D_POOL + D_CONV))
        gt = _dot(hb, win(D_POOL + D_CONV, D_POOL + 2 * D_CONV))
        glu = a * _sigmoid(gt)
        acc = row(convb_ref) + convw_ref[CONV_BUF:CONV_WIDTH, :] * glu
        for r in range(CONV_BUF):
            acc = acc + convw_ref[r:r + 1, :] * sconv_ref[r]
        yb = _silu(_layer_norm(acc, row(clg_ref), row(clb_ref)))
        nconv_ref[0:CONV_BUF - 1] = sconv_ref[1:CONV_BUF]
        nconv_ref[CONV_BUF - 1] = glu

        off_u = D_POOL + 2 * D_CONV
        u = _gelu_tanh(_dot(hb, win(off_u, off_u + D_GMLP)))
        v = _gelu_tanh(_dot(hb, win(off_u + D_GMLP, off_u + 2 * D_GMLP)))
        v = _layer_norm(v, row(glg_ref), row(glb_ref))
        vrow_ref[...] = v
        yc = u * (row(ws0_ref) * v + row(bs0_ref))

        ycat = jnp.concatenate([ya, yb, yc], axis=-1).astype(_BF16)
        x1 = x + mod_ref[2] * _dot(ycat, weight("w_out", wout_ref))
        x1_buf[...] = x1
        h2b[...] = _mod_norm(x1, row(ng2_ref), mod_ref[4], mod_ref[3]).astype(_BF16)
        acc_ref[...] = jnp.zeros(acc_ref.shape, _F32)

    @pl.when(step > 0)
    def _():
        a1 = _dot(h2b[...], weight("w_ff1", w1_ref))
        a3 = _dot(h2b[...], weight("w_ff3", w3_ref))
        act = (_silu(a1) * a3).astype(_BF16)
        acc_ref[...] += _dot(act, weight("w_ff2", w2_ref))

    @pl.when(step == pl.num_programs(0) - 1)
    def _():
        out = x1_buf[...] + mod_ref[5] * acc_ref[...]
        if final_norm:
            out = _rms_norm(out, fg_ref[...])
        o_ref[...] = out


def _layer_sample_call(layer, xs, mod, P, W, state_pool, state_conv, final_norm, cast):
    n = xs.shape[0]
    depth = P["norm1_g"].shape[0]
    sp = _param_specs(layer, depth)
    ff_chunk = FF_CHUNK if cast else D_FF
    n_chunks = D_FF // ff_chunk
    chunk = lambda s: jnp.maximum(s - 1, 0)

    def w_spec(shape, index, **kw):
        if cast:
            return pl.BlockSpec((None,) + shape, lambda s: (layer,) + index(s), **kw)
        return pl.BlockSpec(shape, index, **kw)

    once = dict(pipeline_mode=pl.Buffered(1))
    w_specs = dict(
        w_in=w_spec((D_MODEL, D_IN), lambda s: (0, 0), **once),
        w_out=w_spec((D_MODEL, D_MODEL), lambda s: (0, 0), **once),
    )
    scratch = [pltpu.VMEM((n, D_MODEL), _F32), pltpu.VMEM((n, D_MODEL), _BF16),
               pltpu.VMEM((n, D_MODEL), _F32)]
    if cast:
        w_specs.update(
            w_ff1=w_spec((D_MODEL, ff_chunk), lambda s: (0, chunk(s))),
            w_ff3=w_spec((D_MODEL, ff_chunk), lambda s: (0, chunk(s))),
            w_ff2=w_spec((ff_chunk, D_MODEL), lambda s: (chunk(s), 0)),
        )
    else:
        w_specs.update({name: pl.BlockSpec(memory_space=pl.ANY) for name in ("w_ff1", "w_ff3", "w_ff2")})
        scratch += [pltpu.VMEM((D_MODEL, D_FF), _BF16), pltpu.VMEM((D_MODEL, D_FF), _BF16),
                    pltpu.VMEM((D_FF, D_MODEL), _BF16), pltpu.SemaphoreType.DMA((3,))]
    in_specs = [
        _resident((n, D_MODEL), (0, 0)),
        _resident((None, N_MOD, n, D_MODEL), (layer, 0, 0, 0)),
        sp["ng"], sp["ng"], sp["fg"], w_specs["w_in"], sp["pool_bd"], sp["rows256"], sp["conv_w"],
        sp["rows256"], sp["rows256"], sp["rows256"], sp["rows512"], sp["rows512"], sp["rows512"],
        sp["rows512"], w_specs["w_out"], w_specs["w_ff1"], w_specs["w_ff3"], w_specs["w_ff2"],
        _resident((None, POOL_BUF, n, D_POOL), (layer, 0, 0, 0)),
        _resident((None, CONV_BUF, n, D_CONV), (layer, 0, 0, 0)),
    ]
    assert len(in_specs) == _N_SAMPLE_INPUTS
    out_shape = [
        jax.ShapeDtypeStruct((n, D_MODEL), _F32),
        jax.ShapeDtypeStruct((POOL_BUF, n, D_POOL), _F32),
        jax.ShapeDtypeStruct((CONV_BUF, n, D_CONV), _F32),
        jax.ShapeDtypeStruct((n, D_GMLP), _F32),
    ]
    whole = lambda shape: pl.BlockSpec(shape, lambda s: (0,) * len(shape))
    out_specs = [whole((n, D_MODEL)), whole((POOL_BUF, n, D_POOL)), whole((CONV_BUF, n, D_CONV)),
                 whole((n, D_GMLP))]
    if cast:
        out_shape += [jax.ShapeDtypeStruct(W[name].shape[1:], _BF16) for name in _BIG_WEIGHTS]
        out_specs += [
            whole((D_MODEL, D_IN)), whole((D_MODEL, D_MODEL)),
            pl.BlockSpec((D_MODEL, ff_chunk), lambda s: (0, chunk(s))),
            pl.BlockSpec((D_MODEL, ff_chunk), lambda s: (0, chunk(s))),
            pl.BlockSpec((ff_chunk, D_MODEL), lambda s: (chunk(s), 0)),
        ]
    outs = pl.pallas_call(
        functools.partial(_layer_sample_kernel, layer=layer, final_norm=final_norm, cast=cast),
        out_shape=out_shape,
        grid=(n_chunks + 1,),
        in_specs=in_specs,
        out_specs=out_specs,
        scratch_shapes=scratch,
        compiler_params=pltpu.CompilerParams(
            dimension_semantics=("arbitrary",), vmem_limit_bytes=VMEM_LIMIT),
        name="layer_sample",
    )(xs, mod, P["norm1_g"], P["norm2_g"], P["final_g"], W["w_in"], P["pool_bd"], P["pool_scale"],
      P["conv_w"], P["conv_b"], P["conv_ln_g"], P["conv_ln_b"], P["gmlp_ln_g"], P["gmlp_ln_b"],
      P["ws0"], P["bs0"], W["w_out"], W["w_ff1"], W["w_ff3"], W["w_ff2"], state_pool, state_conv)
    bf16_w = dict(zip(_BIG_WEIGHTS, outs[4:])) if cast else None
    return outs[0], outs[1], outs[2], outs[3], bf16_w


def _prep_params(norm1_g, norm2_g, pool_w, pool_scale, conv_w, conv_b, conv_ln_g, conv_ln_b,
                 gmlp_ln_g, gmlp_ln_b, gmlp_ws, gmlp_bs, final_g):
    depth = pool_w.shape[0]
    n_grp = len(POOL_WINDOWS)
    eye = jnp.eye(n_grp, dtype=_F32)
    pool_bd = (eye[None, :, None, :, None] * pool_w[:, :, :, None, :]).reshape(depth, D_POOL, D_POOL)
    mask = jnp.tril(jnp.ones((CHUNK, CHUNK), dtype=bool))
    ws = jnp.where(mask, gmlp_ws, jnp.zeros_like(gmlp_ws))
    ws_pairs = ws.reshape(depth, N_GMLP_HEADS // 2, 2, CHUNK, CHUNK).transpose(0, 1, 3, 2, 4).reshape(
        depth, N_GMLP_HEADS // 2, CHUNK, 2 * CHUNK)
    bs_full = jnp.repeat(gmlp_bs.transpose(0, 2, 1), GMLP_HEAD, axis=2)
    ws0 = jnp.repeat(gmlp_ws[:, :, 0, 0], GMLP_HEAD, axis=1)
    bs0 = jnp.repeat(gmlp_bs[:, :, 0], GMLP_HEAD, axis=1)
    return dict(
        norm1_g=norm1_g, norm2_g=norm2_g, final_g=final_g.reshape(1, D_MODEL),
        pool_bd=pool_bd.astype(_BF16), pool_scale=pool_scale,
        conv_w=conv_w, conv_b=conv_b, conv_ln_g=conv_ln_g, conv_ln_b=conv_ln_b,
        gmlp_ln_g=gmlp_ln_g, gmlp_ln_b=gmlp_ln_b,
        ws_pairs=ws_pairs.astype(_BF16), bs_full=bs_full, ws0=ws0, bs0=bs0,
    )


def kernel(x_prompt, x_sample, c_prompt, c_sample, state_pool, state_conv, w_ada, b_ada, norm1_g, norm2_g, w_in, pool_w, pool_scale, conv_w, conv_b, conv_ln_g, conv_ln_b, gmlp_ln_g, gmlp_ln_b, gmlp_ws, gmlp_bs, w_out, w_ff1, w_ff3, w_ff2, final_g):
    depth = w_in.shape[0]
    nb = x_prompt.shape[0]
    ns = x_sample.shape[0]
    assert x_sample.shape[1] == 1 and x_prompt.shape[1] % TOKEN_TILE == 0 and ns % nb == 0

    c_all = jnp.concatenate([c_sample, c_prompt], axis=0)
    mod = _ada_call(c_all, w_ada, b_ada)
    P = _prep_params(norm1_g, norm2_g, pool_w, pool_scale, conv_w, conv_b, conv_ln_g, conv_ln_b,
                     gmlp_ln_g, gmlp_ln_b, gmlp_ws, gmlp_bs, final_g)
    big_f32 = dict(w_in=w_in, w_out=w_out, w_ff1=w_ff1, w_ff3=w_ff3, w_ff2=w_ff2)

    xp = x_prompt
    xs = x_sample.reshape(ns, D_MODEL)
    spool_t = state_pool.transpose(0, 2, 1, 3)
    sconv_t = state_conv.transpose(0, 2, 1, 3)
    pool_p, conv_p, pool_s, conv_s, v_s = [], [], [], [], []
    W = None
    for l in range(depth):
        last = l == depth - 1
        xs, npool_s, nconv_s, nv_s, cast_w = _layer_sample_call(
            l, xs, mod, P, big_f32 if W is None else W, spool_t, sconv_t, last, cast=W is None)
        W = cast_w if W is None else W
        pool_s.append(npool_s)
        conv_s.append(nconv_s)
        v_s.append(nv_s.reshape(ns, 1, D_GMLP))
        xp, npool_p, nconv_p, W, stacks_s = _layer_prompt_call(
            l, xp, mod, ns, P, W, None if last else big_f32, last, (pool_s, conv_s) if last else None)
        pool_p.append(npool_p)
        conv_p.append(nconv_p)
    return (xp, xs.reshape(ns, 1, D_MODEL), jnp.stack(pool_p), jnp.stack(conv_p),
            stacks_s[0].transpose(0, 2, 1, 3), stacks_s[1].transpose(0, 2, 1, 3), jnp.stack(v_s))
```

```python
import functools

import jax
import jax.numpy as jnp
from jax.experimental import pallas as pl
from jax.experimental.pallas import tpu as pltpu

D_MODEL = 1024
D_POOL = 256
POOL_WINDOWS = (2, 4, 8, 16)
POOL_GROUP = 64
POOL_BUF = 15
D_CONV = 256
CONV_WIDTH = 31
CONV_BUF = 30
D_GMLP = 512
GMLP_HEAD = 64
N_GMLP_HEADS = 8
CHUNK = 128
D_IN = D_POOL + 2 * D_CONV + 2 * D_GMLP
D_FF = 2816
N_MOD = 6
EPS = 1e-6

LANES = 128
SUBLANES = 8

TOKEN_TILE = 512
FF_CHUNK = 256
ADA_TERMS = 2
POOL_HIST = 16
CONV_HIST = 32
VMEM_LIMIT = 60 * 1024 * 1024

_BF16 = jnp.bfloat16
_F32 = jnp.float32


def _dot(a, b):
    return jnp.dot(a, b, preferred_element_type=_F32)


_LOG2E = 1.4426950408889634


def _sigmoid(x):
    return 1.0 / (1.0 + jnp.exp2(x * -_LOG2E))


def _silu(x):
    return x * _sigmoid(x)


def _gelu_tanh(x):
    k1 = -2.0 * 0.7978845608028654 * _LOG2E
    k3 = k1 * 0.044715
    return x / (1.0 + jnp.exp2(x * (k1 + k3 * (x * x))))


def _rms_norm(x, g):
    ms = jnp.mean(x * x, axis=-1, keepdims=True)
    return (x * jax.lax.rsqrt(ms + EPS)) * g


def _mod_norm(x, g, sc, sh):
    ms = jnp.mean(x * x, axis=-1, keepdims=True)
    return (x * jax.lax.rsqrt(ms + EPS)) * (g * (1.0 + sc)) + sh


def _layer_norm(x, g, b):
    mu = jnp.mean(x, axis=-1, keepdims=True)
    xc = x - mu
    var = jnp.mean(xc * xc, axis=-1, keepdims=True)
    return xc * jax.lax.rsqrt(var + EPS) * g + b


def _lo_lanes():
    return jax.lax.broadcasted_iota(jnp.int32, (1, LANES), 1) < POOL_GROUP


def _ada_kernel(c_ref, w_ref, b_ref, o_ref):
    l = pl.program_id(0)
    s = _silu(c_ref[...]).astype(_BF16)
    res = _dot(s, w_ref[...].astype(_BF16)) + b_ref[pl.ds(l, 1), :]
    for k in range(ADA_TERMS):
        o_ref[k] = res[:, k * D_MODEL:(k + 1) * D_MODEL]


def _ada_call(c_all, w_ada, b_ada):
    depth = w_ada.shape[0]
    n = c_all.shape[0]
    cols = ADA_TERMS * D_MODEL
    return pl.pallas_call(
        _ada_kernel,
        out_shape=jax.ShapeDtypeStruct((depth, N_MOD, n, D_MODEL), _F32),
        grid=(depth, N_MOD // ADA_TERMS),
        in_specs=[
            pl.BlockSpec((n, D_MODEL), lambda l, j: (0, 0)),
            pl.BlockSpec((None, D_MODEL, cols), lambda l, j: (l, 0, j)),
            pl.BlockSpec((depth, cols), lambda l, j: (0, j)),
        ],
        out_specs=pl.BlockSpec((None, ADA_TERMS, n, D_MODEL), lambda l, j: (l, j, 0, 0)),
        compiler_params=pltpu.CompilerParams(
            dimension_semantics=("arbitrary", "arbitrary"), vmem_limit_bytes=VMEM_LIMIT),
        name="ada_mod",
    )(c_all, w_ada, b_ada)


def _pool_mix(means, xa, poolw_ref, pools):
    d = jnp.concatenate(means, axis=-1) - xa
    return _dot(d.astype(_BF16), poolw_ref[...]) * pools


def _mixer_prompt_stages(x_ref, sh, sc, g, ng, t, win_ref, poolw_ref, pools, convw_ref, convb, clg, clb,
                         glg, glb, wsp_ref, bsf_ref, wout_ref, xa_ext, glu_ext, ycat, finish):
    T = TOKEN_TILE
    x = x_ref[...]
    h = _mod_norm(x, ng, sc, sh)
    hb = h.astype(_BF16)
    lo = _lo_lanes()
    yield

    xa = _dot(hb, win_ref[:, 0:D_POOL])
    a = _dot(hb, win_ref[:, D_POOL:D_POOL + D_CONV])
    gt = _dot(hb, win_ref[:, D_POOL + D_CONV:D_POOL + 2 * D_CONV])
    yield
    xa_ext[POOL_HIST:POOL_HIST + T, :] = xa
    glu_ext[CONV_HIST:CONV_HIST + T, :] = a * _sigmoid(gt)
    pos1 =(t * T + 1 + jax.lax.broadcasted_iota(jnp.int32, (T, LANES), 0)).astype(_F32)
    means = []
    for half in range(2):
        w_a, w_b = POOL_WINDOWS[2 * half], POOL_WINDOWS[2 * half + 1]
        cur = xa_ext[:, half * LANES:(half + 1) * LANES]
        sums = {1: cur}
        w = 1
        while w < w_b:
            cur = cur + pltpu.roll(cur, w, axis=0)
            w *= 2
            sums[w] = cur
        s_a = sums[w_a][POOL_HIST:POOL_HIST + T, :]
        s_b = sums[w_b][POOL_HIST:POOL_HIST + T, :]
        cnt = jnp.minimum(pos1, jnp.where(lo, float(w_a), float(w_b)))
        means.append(jnp.where(lo, s_a, s_b) / cnt)
    ya = _pool_mix(means, xa, poolw_ref, pools)
    ycat[:, 0:D_POOL] = ya.astype(_BF16)
    yield

    off_u = D_POOL + 2 * D_CONV
    u = _gelu_tanh(_dot(hb, win_ref[:, off_u:off_u + D_GMLP]))
    v = _gelu_tanh(_dot(hb, win_ref[:, off_u + D_GMLP:off_u + 2 * D_GMLP]))
    v = _layer_norm(v, glg, glb)
    yield

    lead = CONV_HIST - CONV_BUF
    acc = convb
    for r in range(SUBLANES):
        part = None
        for q in range((lead + CONV_WIDTH - 1) // SUBLANES + 1):
            j = SUBLANES * q + r
            if lead <= j < lead + CONV_WIDTH:
                term = convw_ref[j - lead:j - lead + 1, :] * glu_ext[SUBLANES * q:SUBLANES * q + T + SUBLANES, :]
                part = term if part is None else part + term
        acc = acc + part[r:r + T, :]
    yb = _silu(_layer_norm(acc, clg, clb))
    ycat[:, D_POOL:D_POOL + D_CONV] = yb.astype(_BF16)
    yield

    zero = jnp.zeros((CHUNK, LANES), _F32)
    for cp in range(T // (2 * CHUNK)):
        r0 = cp * 2 * CHUNK
        for m in range(N_GMLP_HEADS // 2):
            l0 = m * LANES
            v0 = v[r0:r0 + CHUNK, l0:l0 + LANES]
            v1 = v[r0 + CHUNK:r0 + 2 * CHUNK, l0:l0 + LANES]
            top = jnp.concatenate([jnp.where(lo, v0, zero), jnp.where(lo, v1, zero)], axis=1)
            bot = jnp.concatenate([jnp.where(lo, zero, v0), jnp.where(lo, zero, v1)], axis=1)
            rhs = jnp.concatenate([top, bot], axis=0).astype(_BF16)
            z = _dot(wsp_ref[m], rhs)
            bias = bsf_ref[:, l0:l0 + LANES]
            y0 = u[r0:r0 + CHUNK, l0:l0 + LANES] * (z[:, 0:LANES] + bias)
            y1 = u[r0 + CHUNK:r0 + 2 * CHUNK, l0:l0 + LANES] * (z[:, LANES:2 * LANES] + bias)
            c0 = D_POOL + D_CONV + l0
            ycat[r0:r0 + CHUNK, c0:c0 + LANES] = y0.astype(_BF16)
            ycat[r0 + CHUNK:r0 + 2 * CHUNK, c0:c0 + LANES] = y1.astype(_BF16)
    yield

    y_lo = _dot(ycat[...], wout_ref[:, 0:D_MODEL // 2])
    yield
    y_hi = _dot(ycat[...], wout_ref[:, D_MODEL // 2:D_MODEL])
    finish(x + g * jnp.concatenate([y_lo, y_hi], axis=-1))
    xa_ext[0:POOL_HIST, :] = xa_ext[T:T + POOL_HIST, :]
    glu_ext[0:CONV_HIST, :] = glu_ext[T:T + CONV_HIST, :]
    yield


def _ffn_prompt_stages(o_ref, x1_buf, h2b, g, w1_ref, w3_ref, w2_ref, fg, act, final_norm):
    o_ref[...] = x1_buf[...]
    yield
    for c in range(D_FF // FF_CHUNK):
        cols = slice(c * FF_CHUNK, (c + 1) * FF_CHUNK)
        a1 = _dot(h2b[...], w1_ref[:, cols])
        a3 = _dot(h2b[...], w3_ref[:, cols])
        act[:, cols] = (_silu(a1) * a3).astype(_BF16)
        yield
    half = D_MODEL // 2
    lo = slice(0, half)
    o_ref[:, lo] = o_ref[:, lo] + g[:, lo] * _dot(act[...], w2_ref[:, lo])
    yield
    hi = slice(half, D_MODEL)
    out_hi = o_ref[:, hi] + g[:, hi] * _dot(act[...], w2_ref[:, hi])
    if final_norm:
        o_ref[...] = _rms_norm(jnp.concatenate([o_ref[:, lo], out_hi], axis=-1), fg)
    else:
        o_ref[:, hi] = out_hi
    yield


_STAGE_ORDER = "ba" + "bbba" + "bba" + "bba" + "bb" + "aa" + "bba" + "ba" + "b"
_CAST_AFTER_STAGE = 3
_N_PROMPT_INPUTS = 20
_BIG_WEIGHTS = ("w_in", "w_out", "w_ff1", "w_ff3", "w_ff2")


def _layer_prompt_kernel(*refs, layer, n_t, n_tiles, final_norm, n_cast, n_stack):
    n_in = _N_PROMPT_INPUTS + n_cast + 2 * n_stack
    n_out = 3 + n_cast + (2 if n_stack else 0)
    (x_ref, mod_ref, ng1_ref, ng2_ref, fg_ref, win_ref, poolw_ref, pools_ref, convw_ref, convb_ref,
     clg_ref, clb_ref, glg_ref, glb_ref, wsp_ref, bsf_ref, wout_ref, w1_hbm, w3_hbm,
     w2_hbm) = refs[:_N_PROMPT_INPUTS]
    cast_src = refs[_N_PROMPT_INPUTS:_N_PROMPT_INPUTS + n_cast]
    stack_src = refs[_N_PROMPT_INPUTS + n_cast:n_in]
    o_ref, npool_ref, nconv_ref = refs[n_in:n_in + 3]
    cast_dst = refs[n_in + 3:n_in + 3 + n_cast]
    stack_dst = refs[n_in + 3 + n_cast:n_in + n_out]
    (xa_ext, glu_ext, ycat, act, x1_buf, h2b, w1_ref, w3_ref, w2_ref,
     ffn_sem, *stack_sem) = refs[n_in + n_out:]
    T = TOKEN_TILE
    s = pl.program_id(0)
    tile_a = jnp.minimum(s, n_tiles - 1)
    n_a = tile_a // n_t
    t_a = tile_a - n_a * n_t
    n_b = jnp.maximum(s - 1, 0) // n_t
    row = lambda ref: ref[layer:layer + 1, :]
    mod = lambda k, n: mod_ref[k, pl.ds(n, 1), :]

    @pl.when(t_a == 0)
    def _():
        xa_ext[0:POOL_HIST, :] = jnp.zeros((POOL_HIST, D_POOL), _F32)
        glu_ext[0:CONV_HIST, :] = jnp.zeros((CONV_HIST, D_CONV), _F32)
        glu_ext[CONV_HIST + T:CONV_HIST + T + SUBLANES, :] = jnp.zeros((SUBLANES, D_CONV), _F32)

    def finish_mixer(x1):
        x1_buf[...] = x1
        h2 = _mod_norm(x1, row(ng2_ref), mod(4, n_a), mod(3, n_a))
        h2b[...] = h2.astype(_BF16)

    def run(order):
        stages = {
            "a": _mixer_prompt_stages(
                x_ref, mod(0, n_a), mod(1, n_a), mod(2, n_a), row(ng1_ref), t_a, win_ref, poolw_ref,
                row(pools_ref), convw_ref, row(convb_ref), row(clg_ref), row(clb_ref), row(glg_ref),
                row(glb_ref), wsp_ref, bsf_ref, wout_ref, xa_ext, glu_ext, ycat, finish_mixer),
            "b": _ffn_prompt_stages(o_ref, x1_buf, h2b, mod(5, n_b), w1_ref, w3_ref, w2_ref,
                                    fg_ref[...], act, final_norm),
        }
        for i, which in enumerate(order):
            next(stages[which])
            if i == _CAST_AFTER_STAGE:
                for src, dst in zip(cast_src, cast_dst):
                    dst[...] = src[...].astype(_BF16)
        for which in set(order):
            assert next(stages[which], "done") == "done"

    ffn_copies = [pltpu.make_async_copy(src, dst, ffn_sem.at[i]) for i, (src, dst) in enumerate(
        ((w1_hbm, w1_ref), (w3_hbm, w3_ref), (w2_hbm, w2_ref)))]
    stack_copies = [
        pltpu.make_async_copy(src, stack_dst[i // n_stack].at[i % n_stack], stack_sem[0].at[i])
        for i, src in enumerate(stack_src)]

    @pl.when(s == 0)
    def _():
        for copy in ffn_copies + stack_copies:
            copy.start()

    @pl.when(s == 1)
    def _():
        for copy in ffn_copies:
            copy.wait()

    pl.when(s == 0)(lambda: run(_STAGE_ORDER.replace("b", "")))
    pl.when(jnp.logical_and(s > 0, s < n_tiles))(lambda: run(_STAGE_ORDER))
    pl.when(s == n_tiles)(lambda: run(_STAGE_ORDER.replace("a", "")))

    @pl.when(jnp.logical_and(t_a == n_t - 1, s < n_tiles))
    def _():
        npool_ref[...] = xa_ext[pl.ds(POOL_HIST + T - POOL_BUF, POOL_BUF), :]
        nconv_ref[...] = glu_ext[pl.ds(CONV_HIST + T - CONV_BUF, CONV_BUF), :]

    @pl.when(s == n_tiles)
    def _():
        for copy in stack_copies:
            copy.wait()


def _resident(shape, index):
    return pl.BlockSpec(shape, lambda *_: index, pipeline_mode=pl.Buffered(1))


def _param_specs(layer, depth):
    per_layer_rows = _resident((depth, D_MODEL), (0, 0))
    return dict(
        ng=per_layer_rows,
        fg=_resident((1, D_MODEL), (0, 0)),
        w_in=_resident((D_MODEL, D_IN), (0, 0)),
        pool_bd=_resident((None, D_POOL, D_POOL), (layer, 0, 0)),
        rows256=_resident((depth, D_POOL), (0, 0)),
        conv_w=_resident((None, CONV_WIDTH, D_CONV), (layer, 0, 0)),
        rows512=_resident((depth, D_GMLP), (0, 0)),
        ws_pairs=_resident((None, N_GMLP_HEADS // 2, CHUNK, 2 * CHUNK), (layer, 0, 0, 0)),
        bs_full=_resident((None, CHUNK, D_GMLP), (layer, 0, 0)),
        w_out=_resident((D_MODEL, D_MODEL), (0, 0)),
    )


def _cast_rows(n_rows, n_steps):
    bf16_rows = 2 * SUBLANES
    for rows in range(bf16_rows, n_rows + 1, bf16_rows):
        if n_rows % rows == 0 and n_rows // rows <= n_steps:
            return rows
    raise ValueError((n_rows, n_steps))


def _layer_prompt_call(layer, x, mod, n_sample, P, W, next_f32, final_norm, stack_states):
    nb, seq, _ = x.shape
    depth = P["norm1_g"].shape[0]
    T = TOKEN_TILE
    n_t = seq // T
    n_tiles = nb * n_t
    n_steps = n_tiles + 1
    sp = _param_specs(layer, depth)
    cast_in, cast_specs_in, cast_shapes, cast_specs_out = [], [], [], []
    if next_f32 is not None:
        for name in _BIG_WEIGHTS:
            w = next_f32[name]
            n_rows, n_cols = w.shape[1:]
            rows = _cast_rows(n_rows, n_steps)
            last = n_rows // rows - 1
            cast_in.append(w)
            cast_specs_in.append(pl.BlockSpec(
                (None, rows, n_cols), lambda s, last=last: (layer + 1, jnp.minimum(s, last), 0)))
            cast_shapes.append(jax.ShapeDtypeStruct((n_rows, n_cols), _BF16))
            cast_specs_out.append(pl.BlockSpec(
                (rows, n_cols), lambda s, last=last: (jnp.minimum(s, last), 0)))
    stack_in, stack_shapes, stack_scratch = (), [], []
    if stack_states is not None:
        n_stack = len(stack_states[0])
        assert len(stack_states[1]) == n_stack
        stack_in = tuple(stack_states[0]) + tuple(stack_states[1])
        stack_shapes = [jax.ShapeDtypeStruct((n_stack,) + group[0].shape, _F32) for group in stack_states]
        stack_scratch = [pltpu.SemaphoreType.DMA((len(stack_in),))]
    in_hbm = pl.BlockSpec(memory_space=pl.ANY)

    def x_map(s):
        tile = jnp.minimum(s, n_tiles - 1)
        return (tile // n_t, tile % n_t, 0)

    def o_map(s):
        tile = jnp.maximum(s - 1, 0)
        return (tile // n_t, tile % n_t, 0)

    def state_map(s):
        return (jnp.minimum(s, n_tiles - 1) // n_t, 0, 0)

    in_specs = [
        pl.BlockSpec((None, T, D_MODEL), x_map),
        pl.BlockSpec((None, N_MOD, nb, D_MODEL), lambda s: (layer, 0, n_sample // nb, 0),
                     pipeline_mode=pl.Buffered(1)),
        sp["ng"], sp["ng"], sp["fg"], sp["w_in"], sp["pool_bd"], sp["rows256"], sp["conv_w"],
        sp["rows256"], sp["rows256"], sp["rows256"], sp["rows512"], sp["rows512"], sp["ws_pairs"],
        sp["bs_full"], sp["w_out"],
        pl.BlockSpec(memory_space=pl.ANY), pl.BlockSpec(memory_space=pl.ANY),
        pl.BlockSpec(memory_space=pl.ANY),
    ]
    assert len(in_specs) == _N_PROMPT_INPUTS
    out_shape = [
        jax.ShapeDtypeStruct((nb, seq, D_MODEL), _F32),
        jax.ShapeDtypeStruct((nb, POOL_BUF, D_POOL), _F32),
        jax.ShapeDtypeStruct((nb, CONV_BUF, D_CONV), _F32),
    ]
    out_specs = [
        pl.BlockSpec((None, T, D_MODEL), o_map),
        pl.BlockSpec((None, POOL_BUF, D_POOL), state_map),
        pl.BlockSpec((None, CONV_BUF, D_CONV), state_map),
    ]
    outs = pl.pallas_call(
        functools.partial(_layer_prompt_kernel, layer=layer, n_t=n_t, n_tiles=n_tiles,
                          final_norm=final_norm, n_cast=len(cast_in), n_stack=len(stack_in) // 2),
        out_shape=out_shape + cast_shapes + stack_shapes,
        grid=(n_steps,),
        in_specs=in_specs + cast_specs_in + [in_hbm] * len(stack_in),
        out_specs=out_specs + cast_specs_out + [in_hbm] * len(stack_shapes),
        scratch_shapes=[
            pltpu.VMEM((T + POOL_HIST, D_POOL), _F32),
            pltpu.VMEM((T + CONV_HIST + SUBLANES, D_CONV), _F32),
            pltpu.VMEM((T, D_MODEL), _BF16),
            pltpu.VMEM((T, D_FF), _BF16),
            pltpu.VMEM((T, D_MODEL), _F32),
            pltpu.VMEM((T, D_MODEL), _BF16),
            pltpu.VMEM((D_MODEL, D_FF), _BF16),
            pltpu.VMEM((D_MODEL, D_FF), _BF16),
            pltpu.VMEM((D_FF, D_MODEL), _BF16),
            pltpu.SemaphoreType.DMA((3,)),
        ] + stack_scratch,
        compiler_params=pltpu.CompilerParams(
            dimension_semantics=("arbitrary",), vmem_limit_bytes=VMEM_LIMIT),
        name="layer_prompt",
    )(x, mod, P["norm1_g"], P["norm2_g"], P["final_g"], W["w_in"], P["pool_bd"], P["pool_scale"],
      P["conv_w"], P["conv_b"], P["conv_ln_g"], P["conv_ln_b"], P["gmlp_ln_g"], P["gmlp_ln_b"],
      P["ws_pairs"], P["bs_full"], W["w_out"], W["w_ff1"], W["w_ff3"], W["w_ff2"], *cast_in,
      *stack_in)
    next_w = dict(zip(_BIG_WEIGHTS, outs[3:])) if cast_in else None
    stacks = tuple(outs[3 + len(cast_in):]) if stack_in else None
    return outs[0], outs[1], outs[2], next_w, stacks


_N_SAMPLE_INPUTS = 22
_WEIGHT_RING = 5


def _layer_sample_kernel(*refs, layer, final_norm, cast):
    (x_ref, mod_ref, ng1_ref, ng2_ref, fg_ref, win_ref, poolw_ref, pools_ref, convw_ref, convb_ref,
     clg_ref, clb_ref, glg_ref, glb_ref, ws0_ref, bs0_ref, wout_ref, w1_ref, w3_ref, w2_ref,
     spool_ref, sconv_ref) = refs[:_N_SAMPLE_INPUTS]
    o_ref, npool_ref, nconv_ref, vrow_ref = refs[_N_SAMPLE_INPUTS:_N_SAMPLE_INPUTS + 4]
    step = pl.program_id(0)
    row = lambda ref: ref[layer:layer + 1, :]
    if cast:
        copies = dict(zip(_BIG_WEIGHTS, refs[_N_SAMPLE_INPUTS + 4:-7]))
        x1_buf, h2b, acc_ref, w1_ring, w3_ring, w2_ring, ring_sem = refs[-7:]
        ff_chunk = w2_ring.shape[1]
        n_chunks = D_FF // ff_chunk
        w1_hbm, w3_hbm, w2_hbm = w1_ref, w3_ref, w2_ref

        def ring_copies(k):
            slot = k % _WEIGHT_RING
            off = k * ff_chunk if isinstance(k, int) else pl.multiple_of(k * ff_chunk, ff_chunk)
            return [
                pltpu.make_async_copy(w1_hbm.at[layer, :, pl.ds(off, ff_chunk)], w1_ring.at[slot],
                                      ring_sem.at[0, slot]),
                pltpu.make_async_copy(w3_hbm.at[layer, :, pl.ds(off, ff_chunk)], w3_ring.at[slot],
                                      ring_sem.at[1, slot]),
                pltpu.make_async_copy(w2_hbm.at[layer, pl.ds(off, ff_chunk), :], w2_ring.at[slot],
                                      ring_sem.at[2, slot])]

        @pl.when(step == 0)
        def _():
            for k in range(min(_WEIGHT_RING, n_chunks)):
                for copy in ring_copies(k):
                    copy.start()

        this_chunk = jnp.maximum(step - 1, 0)

        @pl.when(step > 0)
        def _():
            for copy in ring_copies(this_chunk):
                copy.wait()

        this_slot = this_chunk % _WEIGHT_RING
        w1_ref, w3_ref, w2_ref = w1_ring.at[this_slot], w3_ring.at[this_slot], w2_ring.at[this_slot]
    else:
        x1_buf, h2b, acc_ref, w1_vmem, w3_vmem, w2_vmem, ffn_sem = refs[-7:]
        ffn_copies = [pltpu.make_async_copy(src, dst, ffn_sem.at[i]) for i, (src, dst) in enumerate(
            ((w1_ref, w1_vmem), (w3_ref, w3_vmem), (w2_ref, w2_vmem)))]
        w1_ref, w3_ref, w2_ref = w1_vmem, w3_vmem, w2_vmem

        @pl.when(step == 0)
        def _():
            for copy in ffn_copies:
                copy.start()

        @pl.when(step == 1)
        def _():
            for copy in ffn_copies:
                copy.wait()

    def weight(name, ref, cols=slice(None)):
        w = ref[:, cols]
        if cast:
            w = w.astype(_BF16)
            copies[name][:, cols] = w
        return w

    @pl.when(step == 0)
    def _():
        x = x_ref[...]
        h = _mod_norm(x, row(ng1_ref), mod_ref[1], mod_ref[0])
        hb = h.astype(_BF16)
        lo = _lo_lanes()
        win = lambda c0, c1: weight("w_in", win_ref, slice(c0, c1))

        xa = _dot(hb, win(0, D_POOL))
        means = []
        for half in range(2):
            cols = slice(half * LANES, (half + 1) * LANES)
            w_a, w_b = POOL_WINDOWS[2 * half], POOL_WINDOWS[2 * half + 1]
            s = xa[:, cols]
            for j in range(1, w_a):
                s = s + spool_ref[POOL_BUF - j, :, cols]
            s_a = s
            for j in range(w_a, w_b):
                s = s + spool_ref[POOL_BUF - j, :, cols]
            means.append(jnp.where(lo, s_a, s) / jnp.where(lo, float(w_a), float(w_b)))
        ya = _pool_mix(means, xa, poolw_ref, row(pools_ref))
        npool_ref[0:POOL_BUF - 1] = spool_ref[1:POOL_BUF]
        npool_ref[POOL_BUF - 1] = xa

        a = _dot(hb, win(D_POOL, D_POOL + D_CONV))
        gt = _dot(hb, win(D_POOL + D_CONV, D_POOL + 2 * D_CONV))
        glu = a * _sigmoid(gt)
        acc = row(convb_ref) + convw_ref[CONV_BUF:CONV_WIDTH, :] * glu
        for r in range(CONV_BUF):
            acc = acc + convw_ref[r:r + 1, :] * sconv_ref[r]
        yb = _silu(_layer_norm(acc, row(clg_ref), row(clb_ref)))
        nconv_ref[0:CONV_BUF - 1] = sconv_ref[1:CONV_BUF]
        nconv_ref[CONV_BUF - 1] = glu

        off_u = D_POOL + 2 * D_CONV
        u = _gelu_tanh(_dot(hb, win(off_u, off_u + D_GMLP)))
        v = _gelu_tanh(_dot(hb, win(off_u + D_GMLP, off_u + 2 * D_GMLP)))
        v = _layer_norm(v, row(glg_ref), row(glb_ref))
        vrow_ref[...] = v
        yc = u * (row(ws0_ref) * v + row(bs0_ref))

        ycat = jnp.concatenate([ya, yb, yc], axis=-1).astype(_BF16)
        x1 = x + mod_ref[2] * _dot(ycat, weight("w_out", wout_ref))
        x1_buf[...] = x1
        h2b[...] = _mod_norm(x1, row(ng2_ref), mod_ref[4], mod_ref[3]).astype(_BF16)
        acc_ref[...] = jnp.zeros(acc_ref.shape, _F32)

    @pl.when(step > 0)
    def _():
        a1 = _dot(h2b[...], weight("w_ff1", w1_ref))
        a3 = _dot(h2b[...], weight("w_ff3", w3_ref))
        act = (_silu(a1) * a3).astype(_BF16)
        acc_ref[...] += _dot(act, weight("w_ff2", w2_ref))
        if cast:
            @pl.when(this_chunk + _WEIGHT_RING < n_chunks)
            def _():
                for copy in ring_copies(this_chunk + _WEIGHT_RING):
                    copy.start()

    @pl.when(step == pl.num_programs(0) - 1)
    def _():
        out = x1_buf[...] + mod_ref[5] * acc_ref[...]
        if final_norm:
            out = _rms_norm(out, fg_ref[...])
        o_ref[...] = out


def _layer_sample_call(layer, xs, mod, P, W, state_pool, state_conv, final_norm, cast):
    n = xs.shape[0]
    depth = P["norm1_g"].shape[0]
    sp = _param_specs(layer, depth)
    ff_chunk = FF_CHUNK if cast else D_FF
    n_chunks = D_FF // ff_chunk
    chunk = lambda s: jnp.maximum(s - 1, 0)

    def w_spec(shape, index, **kw):
        if cast:
            return pl.BlockSpec((None,) + shape, lambda s: (layer,) + index(s), **kw)
        return pl.BlockSpec(shape, index, **kw)

    once = dict(pipeline_mode=pl.Buffered(1))
    w_specs = dict(
        w_in=w_spec((D_MODEL, D_IN), lambda s: (0, 0), **once),
        w_out=w_spec((D_MODEL, D_MODEL), lambda s: (0, 0), **once),
    )
    scratch = [pltpu.VMEM((n, D_MODEL), _F32), pltpu.VMEM((n, D_MODEL), _BF16),
               pltpu.VMEM((n, D_MODEL), _F32)]
    w_specs.update({name: pl.BlockSpec(memory_space=pl.ANY) for name in ("w_ff1", "w_ff3", "w_ff2")})
    if cast:
        scratch += [pltpu.VMEM((_WEIGHT_RING, D_MODEL, ff_chunk), _F32),
                    pltpu.VMEM((_WEIGHT_RING, D_MODEL, ff_chunk), _F32),
                    pltpu.VMEM((_WEIGHT_RING, ff_chunk, D_MODEL), _F32),
                    pltpu.SemaphoreType.DMA((3, _WEIGHT_RING))]
    else:
        scratch += [pltpu.VMEM((D_MODEL, D_FF), _BF16), pltpu.VMEM((D_MODEL, D_FF), _BF16),
                    pltpu.VMEM((D_FF, D_MODEL), _BF16), pltpu.SemaphoreType.DMA((3,))]
    in_specs = [
        _resident((n, D_MODEL), (0, 0)),
        _resident((None, N_MOD, n, D_MODEL), (layer, 0, 0, 0)),
        sp["ng"], sp["ng"], sp["fg"], w_specs["w_in"], sp["pool_bd"], sp["rows256"], sp["conv_w"],
        sp["rows256"], sp["rows256"], sp["rows256"], sp["rows512"], sp["rows512"], sp["rows512"],
        sp["rows512"], w_specs["w_out"], w_specs["w_ff1"], w_specs["w_ff3"], w_specs["w_ff2"],
        _resident((None, POOL_BUF, n, D_POOL), (layer, 0, 0, 0)),
        _resident((None, CONV_BUF, n, D_CONV), (layer, 0, 0, 0)),
    ]
    assert len(in_specs) == _N_SAMPLE_INPUTS
    out_shape = [
        jax.ShapeDtypeStruct((n, D_MODEL), _F32),
        jax.ShapeDtypeStruct((POOL_BUF, n, D_POOL), _F32),
        jax.ShapeDtypeStruct((CONV_BUF, n, D_CONV), _F32),
        jax.ShapeDtypeStruct((n, D_GMLP), _F32),
    ]
    whole = lambda shape: pl.BlockSpec(shape, lambda s: (0,) * len(shape))
    out_specs = [whole((n, D_MODEL)), whole((POOL_BUF, n, D_POOL)), whole((CONV_BUF, n, D_CONV)),
                 whole((n, D_GMLP))]
    if cast:
        out_shape += [jax.ShapeDtypeStruct(W[name].shape[1:], _BF16) for name in _BIG_WEIGHTS]
        out_specs += [
            whole((D_MODEL, D_IN)), whole((D_MODEL, D_MODEL)),
            pl.BlockSpec((D_MODEL, ff_chunk), lambda s: (0, chunk(s))),
            pl.BlockSpec((D_MODEL, ff_chunk), lambda s: (0, chunk(s))),
            pl.BlockSpec((ff_chunk, D_MODEL), lambda s: (chunk(s), 0)),
        ]
    outs = pl.pallas_call(
        functools.partial(_layer_sample_kernel, layer=layer, final_norm=final_norm, cast=cast),
        out_shape=out_shape,
        grid=(n_chunks + 1,),
        in_specs=in_specs,
        out_specs=out_specs,
        scratch_shapes=scratch,
        compiler_params=pltpu.CompilerParams(
            dimension_semantics=("arbitrary",), vmem_limit_bytes=VMEM_LIMIT),
        name="layer_sample",
    )(xs, mod, P["norm1_g"], P["norm2_g"], P["final_g"], W["w_in"], P["pool_bd"], P["pool_scale"],
      P["conv_w"], P["conv_b"], P["conv_ln_g"], P["conv_ln_b"], P["gmlp_ln_g"], P["gmlp_ln_b"],
      P["ws0"], P["bs0"], W["w_out"], W["w_ff1"], W["w_ff3"], W["w_ff2"], state_pool, state_conv)
    bf16_w = dict(zip(_BIG_WEIGHTS, outs[4:])) if cast else None
    return outs[0], outs[1], outs[2], outs[3], bf16_w


def _prep_params(norm1_g, norm2_g, pool_w, pool_scale, conv_w, conv_b, conv_ln_g, conv_ln_b,
                 gmlp_ln_g, gmlp_ln_b, gmlp_ws, gmlp_bs, final_g):
    depth = pool_w.shape[0]
    n_grp = len(POOL_WINDOWS)
    eye = jnp.eye(n_grp, dtype=_F32)
    pool_bd = (eye[None, :, None, :, None] * pool_w[:, :, :, None, :]).reshape(depth, D_POOL, D_POOL)
    mask = jnp.tril(jnp.ones((CHUNK, CHUNK), dtype=bool))
    ws = jnp.where(mask, gmlp_ws, jnp.zeros_like(gmlp_ws))
    ws_pairs = ws.reshape(depth, N_GMLP_HEADS // 2, 2, CHUNK, CHUNK).transpose(0, 1, 3, 2, 4).reshape(
        depth, N_GMLP_HEADS // 2, CHUNK, 2 * CHUNK)
    bs_full = jnp.repeat(gmlp_bs.transpose(0, 2, 1), GMLP_HEAD, axis=2)
    ws0 = jnp.repeat(gmlp_ws[:, :, 0, 0], GMLP_HEAD, axis=1)
    bs0 = jnp.repeat(gmlp_bs[:, :, 0], GMLP_HEAD, axis=1)
    return dict(
        norm1_g=norm1_g, norm2_g=norm2_g, final_g=final_g.reshape(1, D_MODEL),
        pool_bd=pool_bd.astype(_BF16), pool_scale=pool_scale,
        conv_w=conv_w, conv_b=conv_b, conv_ln_g=conv_ln_g, conv_ln_b=conv_ln_b,
        gmlp_ln_g=gmlp_ln_g, gmlp_ln_b=gmlp_ln_b,
        ws_pairs=ws_pairs.astype(_BF16), bs_full=bs_full, ws0=ws0, bs0=bs0,
    )


def kernel(x_prompt, x_sample, c_prompt, c_sample, state_pool, state_conv, w_ada, b_ada, norm1_g, norm2_g, w_in, pool_w, pool_scale, conv_w, conv_b, conv_ln_g, conv_ln_b, gmlp_ln_g, gmlp_ln_b, gmlp_ws, gmlp_bs, w_out, w_ff1, w_ff3, w_ff2, final_g):
    depth = w_in.shape[0]
    nb = x_prompt.shape[0]
    ns = x_sample.shape[0]
    assert x_sample.shape[1] == 1 and x_prompt.shape[1] % TOKEN_TILE == 0 and ns % nb == 0

    c_all = jnp.concatenate([c_sample, c_prompt], axis=0)
    mod = _ada_call(c_all, w_ada, b_ada)
    P = _prep_params(norm1_g, norm2_g, pool_w, pool_scale, conv_w, conv_b, conv_ln_g, conv_ln_b,
                     gmlp_ln_g, gmlp_ln_b, gmlp_ws, gmlp_bs, final_g)
    big_f32 = dict(w_in=w_in, w_out=w_out, w_ff1=w_ff1, w_ff3=w_ff3, w_ff2=w_ff2)

    xp = x_prompt
    xs = x_sample.reshape(ns, D_MODEL)
    spool_t = state_pool.transpose(0, 2, 1, 3)
    sconv_t = state_conv.transpose(0, 2, 1, 3)
    pool_p, conv_p, pool_s, conv_s, v_s = [], [], [], [], []
    W = None
    for l in range(depth):
        last = l == depth - 1
        xs, npool_s, nconv_s, nv_s, cast_w = _layer_sample_call(
            l, xs, mod, P, big_f32 if W is None else W, spool_t, sconv_t, last, cast=W is None)
        W = cast_w if W is None else W
        pool_s.append(npool_s)
        conv_s.append(nconv_s)
        v_s.append(nv_s.reshape(ns, 1, D_GMLP))
        xp, npool_p, nconv_p, W, stacks_s = _layer_prompt_call(
            l, xp, mod, ns, P, W, None if last else big_f32, last, (pool_s, conv_s) if last else None)
        pool_p.append(npool_p)
        conv_p.append(nconv_p)
    return (xp, xs.reshape(ns, 1, D_MODEL), jnp.stack(pool_p), jnp.stack(conv_p),
            stacks_s[0].transpose(0, 2, 1, 3), stacks_s[1].transpose(0, 2, 1, 3), jnp.stack(v_s))
```

```python
import functools

import jax
import jax.numpy as jnp
from jax.experimental import pallas as pl
from jax.experimental.pallas import tpu as pltpu

D_MODEL = 1024
D_POOL = 256
POOL_WINDOWS = (2, 4, 8, 16)
POOL_GROUP = 64
POOL_BUF = 15
D_CONV = 256
CONV_WIDTH = 31
CONV_BUF = 30
D_GMLP = 512
GMLP_HEAD = 64
N_GMLP_HEADS = 8
CHUNK = 128
D_IN = D_POOL + 2 * D_CONV + 2 * D_GMLP
D_FF = 2816
N_MOD = 6
EPS = 1e-6

LANES = 128
SUBLANES = 8

TOKEN_TILE = 512
FF_CHUNK = 256
ADA_TERMS = 2
POOL_HIST = 16
CONV_HIST = 32
VMEM_LIMIT = 60 * 1024 * 1024

_BF16 = jnp.bfloat16
_F32 = jnp.float32


def _dot(a, b):
    return jnp.dot(a, b, preferred_element_type=_F32)


_LOG2E = 1.4426950408889634


def _sigmoid(x):
    return 1.0 / (1.0 + jnp.exp2(x * -_LOG2E))


def _silu(x):
    return x * _sigmoid(x)


def _gelu_tanh(x):
    k1 = -2.0 * 0.7978845608028654 * _LOG2E
    k3 = k1 * 0.044715
    return x / (1.0 + jnp.exp2(x * (k1 + k3 * (x * x))))


def _rms_norm(x, g):
    ms = jnp.mean(x * x, axis=-1, keepdims=True)
    return (x * jax.lax.rsqrt(ms + EPS)) * g


def _mod_norm(x, g, sc, sh):
    ms = jnp.mean(x * x, axis=-1, keepdims=True)
    return (x * jax.lax.rsqrt(ms + EPS)) * (g * (1.0 + sc)) + sh


def _layer_norm(x, g, b):
    mu = jnp.mean(x, axis=-1, keepdims=True)
    xc = x - mu
    var = jnp.mean(xc * xc, axis=-1, keepdims=True)
    return xc * jax.lax.rsqrt(var + EPS) * g + b


def _lo_lanes():
    return jax.lax.broadcasted_iota(jnp.int32, (1, LANES), 1) < POOL_GROUP


def _ada_kernel(c_ref, w_ref, b_ref, o_ref):
    l = pl.program_id(0)
    s = _silu(c_ref[...]).astype(_BF16)
    res = _dot(s, w_ref[...].astype(_BF16)) + b_ref[pl.ds(l, 1), :]
    for k in range(ADA_TERMS):
        o_ref[k] = res[:, k * D_MODEL:(k + 1) * D_MODEL]


def _ada_call(c_all, w_ada, b_ada):
    depth = w_ada.shape[0]
    n = c_all.shape[0]
    cols = ADA_TERMS * D_MODEL
    return pl.pallas_call(
        _ada_kernel,
        out_shape=jax.ShapeDtypeStruct((depth, N_MOD, n, D_MODEL), _F32),
        grid=(depth, N_MOD // ADA_TERMS),
        in_specs=[
            pl.BlockSpec((n, D_MODEL), lambda l, j: (0, 0)),
            pl.BlockSpec((None, D_MODEL, cols), lambda l, j: (l, 0, j)),
            pl.BlockSpec((depth, cols), lambda l, j: (0, j)),
        ],
        out_specs=pl.BlockSpec((None, ADA_TERMS, n, D_MODEL), lambda l, j: (l, j, 0, 0)),
        compiler_params=pltpu.CompilerParams(
            dimension_semantics=("arbitrary", "arbitrary"), vmem_limit_bytes=VMEM_LIMIT),
        name="ada_mod",
    )(c_all, w_ada, b_ada)


def _pool_mix(means, xa, poolw_ref, pools):
    d = jnp.concatenate(means, axis=-1) - xa
    return _dot(d.astype(_BF16), poolw_ref[...]) * pools


def _mixer_prompt_stages(x_ref, sh, sc, g, ng, t, win_ref, poolw_ref, pools, convw_ref, convb, clg, clb,
                         glg, glb, wsp_ref, bsf_ref, wout_ref, xa_ext, glu_ext, ycat, finish):
    T = TOKEN_TILE
    x = x_ref[...]
    h = _mod_norm(x, ng, sc, sh)
    hb = h.astype(_BF16)
    lo = _lo_lanes()
    yield

    xa = _dot(hb, win_ref[:, 0:D_POOL])
    a = _dot(hb, win_ref[:, D_POOL:D_POOL + D_CONV])
    gt = _dot(hb, win_ref[:, D_POOL + D_CONV:D_POOL + 2 * D_CONV])
    yield
    xa_ext[POOL_HIST:POOL_HIST + T, :] = xa
    glu_ext[CONV_HIST:CONV_HIST + T, :] = a * _sigmoid(gt)
    pos1 =(t * T + 1 + jax.lax.broadcasted_iota(jnp.int32, (T, LANES), 0)).astype(_F32)
    means = []
    for half in range(2):
        w_a, w_b = POOL_WINDOWS[2 * half], POOL_WINDOWS[2 * half + 1]
        cur = xa_ext[:, half * LANES:(half + 1) * LANES]
        sums = {1: cur}
        w = 1
        while w < w_b:
            cur = cur + pltpu.roll(cur, w, axis=0)
            w *= 2
            sums[w] = cur
        s_a = sums[w_a][POOL_HIST:POOL_HIST + T, :]
        s_b = sums[w_b][POOL_HIST:POOL_HIST + T, :]
        cnt = jnp.minimum(pos1, jnp.where(lo, float(w_a), float(w_b)))
        means.append(jnp.where(lo, s_a, s_b) / cnt)
    ya = _pool_mix(means, xa, poolw_ref, pools)
    ycat[:, 0:D_POOL] = ya.astype(_BF16)
    yield

    off_u = D_POOL + 2 * D_CONV
    u = _gelu_tanh(_dot(hb, win_ref[:, off_u:off_u + D_GMLP]))
    v = _gelu_tanh(_dot(hb, win_ref[:, off_u + D_GMLP:off_u + 2 * D_GMLP]))
    v = _layer_norm(v, glg, glb)
    yield

    lead = CONV_HIST - CONV_BUF
    acc = convb
    for r in range(SUBLANES):
        part = None
        for q in range((lead + CONV_WIDTH - 1) // SUBLANES + 1):
            j = SUBLANES * q + r
            if lead <= j < lead + CONV_WIDTH:
                term = convw_ref[j - lead:j - lead + 1, :] * glu_ext[SUBLANES * q:SUBLANES * q + T + SUBLANES, :]
                part = term if part is None else part + term
        acc = acc + part[r:r + T, :]
    yb = _silu(_layer_norm(acc, clg, clb))
    ycat[:, D_POOL:D_POOL + D_CONV] = yb.astype(_BF16)
    yield

    zero = jnp.zeros((CHUNK, LANES), _F32)
    for cp in range(T // (2 * CHUNK)):
        r0 = cp * 2 * CHUNK
        for m in range(N_GMLP_HEADS // 2):
            l0 = m * LANES
            v0 = v[r0:r0 + CHUNK, l0:l0 + LANES]
            v1 = v[r0 + CHUNK:r0 + 2 * CHUNK, l0:l0 + LANES]
            top = jnp.concatenate([jnp.where(lo, v0, zero), jnp.where(lo, v1, zero)], axis=1)
            bot = jnp.concatenate([jnp.where(lo, zero, v0), jnp.where(lo, zero, v1)], axis=1)
            rhs = jnp.concatenate([top, bot], axis=0).astype(_BF16)
            z = _dot(wsp_ref[m], rhs)
            bias = bsf_ref[:, l0:l0 + LANES]
            y0 = u[r0:r0 + CHUNK, l0:l0 + LANES] * (z[:, 0:LANES] + bias)
            y1 = u[r0 + CHUNK:r0 + 2 * CHUNK, l0:l0 + LANES] * (z[:, LANES:2 * LANES] + bias)
            c0 = D_POOL + D_CONV + l0
            ycat[r0:r0 + CHUNK, c0:c0 + LANES] = y0.astype(_BF16)
            ycat[r0 + CHUNK:r0 + 2 * CHUNK, c0:c0 + LANES] = y1.astype(_BF16)
    yield

    y_lo = _dot(ycat[...], wout_ref[:, 0:D_MODEL // 2])
    yield
    y_hi = _dot(ycat[...], wout_ref[:, D_MODEL // 2:D_MODEL])
    finish(x + g * jnp.concatenate([y_lo, y_hi], axis=-1))
    xa_ext[0:POOL_HIST, :] = xa_ext[T:T + POOL_HIST, :]
    glu_ext[0:CONV_HIST, :] = glu_ext[T:T + CONV_HIST, :]
    yield


def _ffn_prompt_stages(o_ref, x1_buf, h2b, g, w1_ref, w3_ref, w2_ref, fg, act, final_norm):
    o_ref[...] = x1_buf[...]
    yield
    for c in range(D_FF // FF_CHUNK):
        cols = slice(c * FF_CHUNK, (c + 1) * FF_CHUNK)
        a1 = _dot(h2b[...], w1_ref[:, cols])
        a3 = _dot(h2b[...], w3_ref[:, cols])
        act[:, cols] = (_silu(a1) * a3).astype(_BF16)
        yield
    half = D_MODEL // 2
    lo = slice(0, half)
    o_ref[:, lo] = o_ref[:, lo] + g[:, lo] * _dot(act[...], w2_ref[:, lo])
    yield
    hi = slice(half, D_MODEL)
    out_hi = o_ref[:, hi] + g[:, hi] * _dot(act[...], w2_ref[:, hi])
    if final_norm:
        o_ref[...] = _rms_norm(jnp.concatenate([o_ref[:, lo], out_hi], axis=-1), fg)
    else:
        o_ref[:, hi] = out_hi
    yield


_STAGE_ORDER = "ba" + "bbba" + "bba" + "bba" + "bb" + "aa" + "bba" + "ba" + "b"
_CAST_AFTER_STAGE = 3
_N_PROMPT_INPUTS = 20
_BIG_WEIGHTS = ("w_in", "w_out", "w_ff1", "w_ff3", "w_ff2")


def _layer_prompt_kernel(*refs, layer, n_t, n_tiles, final_norm, n_cast, n_stack):
    n_in = _N_PROMPT_INPUTS + n_cast + 2 * n_stack
    n_out = 3 + n_cast + (2 if n_stack else 0)
    (x_ref, mod_ref, ng1_ref, ng2_ref, fg_ref, win_ref, poolw_ref, pools_ref, convw_ref, convb_ref,
     clg_ref, clb_ref, glg_ref, glb_ref, wsp_ref, bsf_ref, wout_ref, w1_hbm, w3_hbm,
     w2_hbm) = refs[:_N_PROMPT_INPUTS]
    cast_src = refs[_N_PROMPT_INPUTS:_N_PROMPT_INPUTS + n_cast]
    stack_src = refs[_N_PROMPT_INPUTS + n_cast:n_in]
    o_ref, npool_ref, nconv_ref = refs[n_in:n_in + 3]
    cast_dst = refs[n_in + 3:n_in + 3 + n_cast]
    stack_dst = refs[n_in + 3 + n_cast:n_in + n_out]
    (xa_ext, glu_ext, ycat, act, x1_buf, h2b, w1_ref, w3_ref, w2_ref,
     ffn_sem, *stack_sem) = refs[n_in + n_out:]
    T = TOKEN_TILE
    s = pl.program_id(0)
    tile_a = jnp.minimum(s, n_tiles - 1)
    n_a = tile_a // n_t
    t_a = tile_a - n_a * n_t
    n_b = jnp.maximum(s - 1, 0) // n_t
    row = lambda ref: ref[layer:layer + 1, :]
    mod = lambda k, n: mod_ref[k, pl.ds(n, 1), :]

    @pl.when(t_a == 0)
    def _():
        xa_ext[0:POOL_HIST, :] = jnp.zeros((POOL_HIST, D_POOL), _F32)
        glu_ext[0:CONV_HIST, :] = jnp.zeros((CONV_HIST, D_CONV), _F32)
        glu_ext[CONV_HIST + T:CONV_HIST + T + SUBLANES, :] = jnp.zeros((SUBLANES, D_CONV), _F32)

    def finish_mixer(x1):
        x1_buf[...] = x1
        h2 = _mod_norm(x1, row(ng2_ref), mod(4, n_a), mod(3, n_a))
        h2b[...] = h2.astype(_BF16)

    def run(order):
        stages = {
            "a": _mixer_prompt_stages(
                x_ref, mod(0, n_a), mod(1, n_a), mod(2, n_a), row(ng1_ref), t_a, win_ref, poolw_ref,
                row(pools_ref), convw_ref, row(convb_ref), row(clg_ref), row(clb_ref), row(glg_ref),
                row(glb_ref), wsp_ref, bsf_ref, wout_ref, xa_ext, glu_ext, ycat, finish_mixer),
            "b": _ffn_prompt_stages(o_ref, x1_buf, h2b, mod(5, n_b), w1_ref, w3_ref, w2_ref,
                                    fg_ref[...], act, final_norm),
        }
        for i, which in enumerate(order):
            next(stages[which])
            if i == _CAST_AFTER_STAGE:
                for src, dst in zip(cast_src, cast_dst):
                    dst[...] = src[...].astype(_BF16)
        for which in set(order):
            assert next(stages[which], "done") == "done"

    ffn_copies = [pltpu.make_async_copy(src, dst, ffn_sem.at[i]) for i, (src, dst) in enumerate(
        ((w1_hbm, w1_ref), (w3_hbm, w3_ref), (w2_hbm, w2_ref)))]
    stack_copies = [
        pltpu.make_async_copy(src, stack_dst[i // n_stack].at[i % n_stack], stack_sem[0].at[i])
        for i, src in enumerate(stack_src)]

    @pl.when(s == 0)
    def _():
        for copy in ffn_copies + stack_copies:
            copy.start()

    @pl.when(s == 1)
    def _():
        for copy in ffn_copies:
            copy.wait()

    pl.when(s == 0)(lambda: run(_STAGE_ORDER.replace("b", "")))
    pl.when(jnp.logical_and(s > 0, s < n_tiles))(lambda: run(_STAGE_ORDER))
    pl.when(s == n_tiles)(lambda: run(_STAGE_ORDER.replace("a", "")))

    @pl.when(jnp.logical_and(t_a == n_t - 1, s < n_tiles))
    def _():
        npool_ref[...] = xa_ext[pl.ds(POOL_HIST + T - POOL_BUF, POOL_BUF), :]
        nconv_ref[...] = glu_ext[pl.ds(CONV_HIST + T - CONV_BUF, CONV_BUF), :]

    @pl.when(s == n_tiles)
    def _():
        for copy in stack_copies:
            copy.wait()


def _resident(shape, index):
    return pl.BlockSpec(shape, lambda *_: index, pipeline_mode=pl.Buffered(1))


def _param_specs(layer, depth):
    per_layer_rows = _resident((depth, D_MODEL), (0, 0))
    return dict(
        ng=per_layer_rows,
        fg=_resident((1, D_MODEL), (0, 0)),
        w_in=_resident((D_MODEL, D_IN), (0, 0)),
        pool_bd=_resident((None, D_POOL, D_POOL), (layer, 0, 0)),
        rows256=_resident((depth, D_POOL), (0, 0)),
        conv_w=_resident((None, CONV_WIDTH, D_CONV), (layer, 0, 0)),
        rows512=_resident((depth, D_GMLP), (0, 0)),
        ws_pairs=_resident((None, N_GMLP_HEADS // 2, CHUNK, 2 * CHUNK), (layer, 0, 0, 0)),
        bs_full=_resident((None, CHUNK, D_GMLP), (layer, 0, 0)),
        w_out=_resident((D_MODEL, D_MODEL), (0, 0)),
    )


def _cast_rows(n_rows, n_steps):
    bf16_rows = 2 * SUBLANES
    for rows in range(bf16_rows, n_rows + 1, bf16_rows):
        if n_rows % rows == 0 and n_rows // rows <= n_steps:
            return rows
    raise ValueError((n_rows, n_steps))


def _layer_prompt_call(layer, x, mod, n_sample, P, W, next_f32, final_norm, stack_states):
    nb, seq, _ = x.shape
    depth = P["norm1_g"].shape[0]
    T = TOKEN_TILE
    n_t = seq // T
    n_tiles = nb * n_t
    n_steps = n_tiles + 1
    sp = _param_specs(layer, depth)
    cast_in, cast_specs_in, cast_shapes, cast_specs_out = [], [], [], []
    if next_f32 is not None:
        for name in _BIG_WEIGHTS:
            w = next_f32[name]
            n_rows, n_cols = w.shape[1:]
            rows = _cast_rows(n_rows, n_steps)
            last = n_rows // rows - 1
            cast_in.append(w)
            cast_specs_in.append(pl.BlockSpec(
                (None, rows, n_cols), lambda s, last=last: (layer + 1, jnp.minimum(s, last), 0)))
            cast_shapes.append(jax.ShapeDtypeStruct((n_rows, n_cols), _BF16))
            cast_specs_out.append(pl.BlockSpec(
                (rows, n_cols), lambda s, last=last: (jnp.minimum(s, last), 0)))
    stack_in, stack_shapes, stack_scratch = (), [], []
    if stack_states is not None:
        n_stack = len(stack_states[0])
        assert len(stack_states[1]) == n_stack
        stack_in = tuple(stack_states[0]) + tuple(stack_states[1])
        stack_shapes = [jax.ShapeDtypeStruct((n_stack,) + group[0].shape, _F32) for group in stack_states]
        stack_scratch = [pltpu.SemaphoreType.DMA((len(stack_in),))]
    in_hbm = pl.BlockSpec(memory_space=pl.ANY)

    def x_map(s):
        tile = jnp.minimum(s, n_tiles - 1)
        return (tile // n_t, tile % n_t, 0)

    def o_map(s):
        tile = jnp.maximum(s - 1, 0)
        return (tile // n_t, tile % n_t, 0)

    def state_map(s):
        return (jnp.minimum(s, n_tiles - 1) // n_t, 0, 0)

    in_specs = [
        pl.BlockSpec((None, T, D_MODEL), x_map),
        pl.BlockSpec((None, N_MOD, nb, D_MODEL), lambda s: (layer, 0, n_sample // nb, 0),
                     pipeline_mode=pl.Buffered(1)),
        sp["ng"], sp["ng"], sp["fg"], sp["w_in"], sp["pool_bd"], sp["rows256"], sp["conv_w"],
        sp["rows256"], sp["rows256"], sp["rows256"], sp["rows512"], sp["rows512"], sp["ws_pairs"],
        sp["bs_full"], sp["w_out"],
        pl.BlockSpec(memory_space=pl.ANY), pl.BlockSpec(memory_space=pl.ANY),
        pl.BlockSpec(memory_space=pl.ANY),
    ]
    assert len(in_specs) == _N_PROMPT_INPUTS
    out_shape = [
        jax.ShapeDtypeStruct((nb, seq, D_MODEL), _F32),
        jax.ShapeDtypeStruct((nb, POOL_BUF, D_POOL), _F32),
        jax.ShapeDtypeStruct((nb, CONV_BUF, D_CONV), _F32),
    ]
    out_specs = [
        pl.BlockSpec((None, T, D_MODEL), o_map),
        pl.BlockSpec((None, POOL_BUF, D_POOL), state_map),
        pl.BlockSpec((None, CONV_BUF, D_CONV), state_map),
    ]
    outs = pl.pallas_call(
        functools.partial(_layer_prompt_kernel, layer=layer, n_t=n_t, n_tiles=n_tiles,
                          final_norm=final_norm, n_cast=len(cast_in), n_stack=len(stack_in) // 2),
        out_shape=out_shape + cast_shapes + stack_shapes,
        grid=(n_steps,),
        in_specs=in_specs + cast_specs_in + [in_hbm] * len(stack_in),
        out_specs=out_specs + cast_specs_out + [in_hbm] * len(stack_shapes),
        scratch_shapes=[
            pltpu.VMEM((T + POOL_HIST, D_POOL), _F32),
            pltpu.VMEM((T + CONV_HIST + SUBLANES, D_CONV), _F32),
            pltpu.VMEM((T, D_MODEL), _BF16),
            pltpu.VMEM((T, D_FF), _BF16),
            pltpu.VMEM((T, D_MODEL), _F32),
            pltpu.VMEM((T, D_MODEL), _BF16),
            pltpu.VMEM((D_MODEL, D_FF), _BF16),
            pltpu.VMEM((D_MODEL, D_FF), _BF16),
            pltpu.VMEM((D_FF, D_MODEL), _BF16),
            pltpu.SemaphoreType.DMA((3,)),
        ] + stack_scratch,
        compiler_params=pltpu.CompilerParams(
            dimension_semantics=("arbitrary",), vmem_limit_bytes=VMEM_LIMIT),
        name="layer_prompt",
    )(x, mod, P["norm1_g"], P["norm2_g"], P["final_g"], W["w_in"], P["pool_bd"], P["pool_scale"],
      P["conv_w"], P["conv_b"], P["conv_ln_g"], P["conv_ln_b"], P["gmlp_ln_g"], P["gmlp_ln_b"],
      P["ws_pairs"], P["bs_full"], W["w_out"], W["w_ff1"], W["w_ff3"], W["w_ff2"], *cast_in,
      *stack_in)
    next_w = dict(zip(_BIG_WEIGHTS, outs[3:])) if cast_in else None
    stacks = tuple(outs[3 + len(cast_in):]) if stack_in else None
    return outs[0], outs[1], outs[2], next_w, stacks


_N_SAMPLE_INPUTS = 22
_WEIGHT_RING = 5


def _layer_sample_kernel(*refs, layer, final_norm, cast):
    (x_ref, mod_ref, ng1_ref, ng2_ref, fg_ref, win_ref, poolw_ref, pools_ref, convw_ref, convb_ref,
     clg_ref, clb_ref, glg_ref, glb_ref, ws0_ref, bs0_ref, wout_ref, w1_ref, w3_ref, w2_ref,
     spool_ref, sconv_ref) = refs[:_N_SAMPLE_INPUTS]
    o_ref, npool_ref, nconv_ref, vrow_ref = refs[_N_SAMPLE_INPUTS:_N_SAMPLE_INPUTS + 4]
    step = pl.program_id(0)
    row = lambda ref: ref[layer:layer + 1, :]
    if cast:
        copies = dict(zip(_BIG_WEIGHTS, refs[_N_SAMPLE_INPUTS + 4:-7]))
        x1_buf, h2b, acc_ref, w1_ring, w3_ring, w2_ring, ring_sem = refs[-7:]
        ff_chunk = w2_ring.shape[1]
        n_chunks = D_FF // ff_chunk
        w1_hbm, w3_hbm, w2_hbm = w1_ref, w3_ref, w2_ref

        def ring_copies(k):
            slot = k % _WEIGHT_RING
            off = k * ff_chunk if isinstance(k, int) else pl.multiple_of(k * ff_chunk, ff_chunk)
            return [
                pltpu.make_async_copy(w1_hbm.at[layer, :, pl.ds(off, ff_chunk)], w1_ring.at[slot],
                                      ring_sem.at[0, slot]),
                pltpu.make_async_copy(w3_hbm.at[layer, :, pl.ds(off, ff_chunk)], w3_ring.at[slot],
                                      ring_sem.at[1, slot]),
                pltpu.make_async_copy(w2_hbm.at[layer, pl.ds(off, ff_chunk), :], w2_ring.at[slot],
                                      ring_sem.at[2, slot])]

        @pl.when(step == 0)
        def _():
            for k in range(min(_WEIGHT_RING, n_chunks)):
                for copy in ring_copies(k):
                    copy.start()

        this_chunk = jnp.maximum(step - 1, 0)

        @pl.when(step > 0)
        def _():
            for copy in ring_copies(this_chunk):
                copy.wait()

        this_slot = this_chunk % _WEIGHT_RING
        w1_ref, w3_ref, w2_ref = w1_ring.at[this_slot], w3_ring.at[this_slot], w2_ring.at[this_slot]
    else:
        x1_buf, h2b, acc_ref, w1_vmem, w3_vmem, w2_vmem, ffn_sem = refs[-7:]
        ffn_copies = [pltpu.make_async_copy(src, dst, ffn_sem.at[i]) for i, (src, dst) in enumerate(
            ((w1_ref, w1_vmem), (w3_ref, w3_vmem), (w2_ref, w2_vmem)))]
        w1_ref, w3_ref, w2_ref = w1_vmem, w3_vmem, w2_vmem

        @pl.when(step == 0)
        def _():
            for copy in ffn_copies:
                copy.start()

        @pl.when(step == 1)
        def _():
            for copy in ffn_copies[:2]:
                copy.wait()

    def weight(name, ref, cols=slice(None)):
        w = ref[:, cols]
        if cast:
            w = w.astype(_BF16)
            copies[name][:, cols] = w
        return w

    @pl.when(step == 0)
    def _():
        x = x_ref[...]
        h = _mod_norm(x, row(ng1_ref), mod_ref[1], mod_ref[0])
        hb = h.astype(_BF16)
        lo = _lo_lanes()
        win = lambda c0, c1: weight("w_in", win_ref, slice(c0, c1))

        xa = _dot(hb, win(0, D_POOL))
        means = []
        for half in range(2):
            cols = slice(half * LANES, (half + 1) * LANES)
            w_a, w_b = POOL_WINDOWS[2 * half], POOL_WINDOWS[2 * half + 1]
            s = xa[:, cols]
            for j in range(1, w_a):
                s = s + spool_ref[POOL_BUF - j, :, cols]
            s_a = s
            for j in range(w_a, w_b):
                s = s + spool_ref[POOL_BUF - j, :, cols]
            means.append(jnp.where(lo, s_a, s) / jnp.where(lo, float(w_a), float(w_b)))
        ya = _pool_mix(means, xa, poolw_ref, row(pools_ref))
        npool_ref[0:POOL_BUF - 1] = spool_ref[1:POOL_BUF]
        npool_ref[POOL_BUF - 1] = xa

        a = _dot(hb, win(D_POOL, D_POOL + D_CONV))
        gt = _dot(hb, win(D_POOL + D_CONV, D_POOL + 2 * D_CONV))
        glu = a * _sigmoid(gt)
        acc = row(convb_ref) + convw_ref[CONV_BUF:CONV_WIDTH, :] * glu
        for r in range(CONV_BUF):
            acc = acc + convw_ref[r:r + 1, :] * sconv_ref[r]
        yb = _silu(_layer_norm(acc, row(clg_ref), row(clb_ref)))
        nconv_ref[0:CONV_BUF - 1] = sconv_ref[1:CONV_BUF]
        nconv_ref[CONV_BUF - 1] = glu

        off_u = D_POOL + 2 * D_CONV
        u = _gelu_tanh(_dot(hb, win(off_u, off_u + D_GMLP)))
        v = _gelu_tanh(_dot(hb, win(off_u + D_GMLP, off_u + 2 * D_GMLP)))
        v = _layer_norm(v, row(glg_ref), row(glb_ref))
        vrow_ref[...] = v
        yc = u * (row(ws0_ref) * v + row(bs0_ref))

        ycat = jnp.concatenate([ya, yb, yc], axis=-1).astype(_BF16)
        x1 = x + mod_ref[2] * _dot(ycat, weight("w_out", wout_ref))
        x1_buf[...] = x1
        h2b[...] = _mod_norm(x1, row(ng2_ref), mod_ref[4], mod_ref[3]).astype(_BF16)
        acc_ref[...] = jnp.zeros(acc_ref.shape, _F32)

    @pl.when(step > 0)
    def _():
        a1 = _dot(h2b[...], weight("w_ff1", w1_ref))
        a3 = _dot(h2b[...], weight("w_ff3", w3_ref))
        act = (_silu(a1) * a3).astype(_BF16)
        if not cast:
            ffn_copies[2].wait()
        acc_ref[...] += _dot(act, weight("w_ff2", w2_ref))
        if cast:
            @pl.when(this_chunk + _WEIGHT_RING < n_chunks)
            def _():
                for copy in ring_copies(this_chunk + _WEIGHT_RING):
                    copy.start()

    @pl.when(step == pl.num_programs(0) - 1)
    def _():
        out = x1_buf[...] + mod_ref[5] * acc_ref[...]
        if final_norm:
            out = _rms_norm(out, fg_ref[...])
        o_ref[...] = out


def _layer_sample_call(layer, xs, mod, P, W, state_pool, state_conv, final_norm, cast):
    n = xs.shape[0]
    depth = P["norm1_g"].shape[0]
    sp = _param_specs(layer, depth)
    ff_chunk = FF_CHUNK if cast else D_FF
    n_chunks = D_FF // ff_chunk
    chunk = lambda s: jnp.maximum(s - 1, 0)

    def w_spec(shape, index, **kw):
        if cast:
            return pl.BlockSpec((None,) + shape, lambda s: (layer,) + index(s), **kw)
        return pl.BlockSpec(shape, index, **kw)

    once = dict(pipeline_mode=pl.Buffered(1))
    w_specs = dict(
        w_in=w_spec((D_MODEL, D_IN), lambda s: (0, 0), **once),
        w_out=w_spec((D_MODEL, D_MODEL), lambda s: (0, 0), **once),
    )
    scratch = [pltpu.VMEM((n, D_MODEL), _F32), pltpu.VMEM((n, D_MODEL), _BF16),
               pltpu.VMEM((n, D_MODEL), _F32)]
    w_specs.update({name: pl.BlockSpec(memory_space=pl.ANY) for name in ("w_ff1", "w_ff3", "w_ff2")})
    if cast:
        scratch += [pltpu.VMEM((_WEIGHT_RING, D_MODEL, ff_chunk), _F32),
                    pltpu.VMEM((_WEIGHT_RING, D_MODEL, ff_chunk), _F32),
                    pltpu.VMEM((_WEIGHT_RING, ff_chunk, D_MODEL), _F32),
                    pltpu.SemaphoreType.DMA((3, _WEIGHT_RING))]
    else:
        scratch += [pltpu.VMEM((D_MODEL, D_FF), _BF16), pltpu.VMEM((D_MODEL, D_FF), _BF16),
                    pltpu.VMEM((D_FF, D_MODEL), _BF16), pltpu.SemaphoreType.DMA((3,))]
    in_specs = [
        _resident((n, D_MODEL), (0, 0)),
        _resident((None, N_MOD, n, D_MODEL), (layer, 0, 0, 0)),
        sp["ng"], sp["ng"], sp["fg"], w_specs["w_in"], sp["pool_bd"], sp["rows256"], sp["conv_w"],
        sp["rows256"], sp["rows256"], sp["rows256"], sp["rows512"], sp["rows512"], sp["rows512"],
        sp["rows512"], w_specs["w_out"], w_specs["w_ff1"], w_specs["w_ff3"], w_specs["w_ff2"],
        _resident((None, POOL_BUF, n, D_POOL), (layer, 0, 0, 0)),
        _resident((None, CONV_BUF, n, D_CONV), (layer, 0, 0, 0)),
    ]
    assert len(in_specs) == _N_SAMPLE_INPUTS
    out_shape = [
        jax.ShapeDtypeStruct((n, D_MODEL), _F32),
        jax.ShapeDtypeStruct((POOL_BUF, n, D_POOL), _F32),
        jax.ShapeDtypeStruct((CONV_BUF, n, D_CONV), _F32),
        jax.ShapeDtypeStruct((n, D_GMLP), _F32),
    ]
    whole = lambda shape: pl.BlockSpec(shape, lambda s: (0,) * len(shape))
    out_specs = [whole((n, D_MODEL)), whole((POOL_BUF, n, D_POOL)), whole((CONV_BUF, n, D_CONV)),
                 whole((n, D_GMLP))]
    if cast:
        out_shape += [jax.ShapeDtypeStruct(W[name].shape[1:], _BF16) for name in _BIG_WEIGHTS]
        out_specs += [
            whole((D_MODEL, D_IN)), whole((D_MODEL, D_MODEL)),
            pl.BlockSpec((D_MODEL, ff_chunk), lambda s: (0, chunk(s))),
            pl.BlockSpec((D_MODEL, ff_chunk), lambda s: (0, chunk(s))),
            pl.BlockSpec((ff_chunk, D_MODEL), lambda s: (chunk(s), 0)),
        ]
    outs = pl.pallas_call(
        functools.partial(_layer_sample_kernel, layer=layer, final_norm=final_norm, cast=cast),
        out_shape=out_shape,
        grid=(n_chunks + 1,),
        in_specs=in_specs,
        out_specs=out_specs,
        scratch_shapes=scratch,
        compiler_params=pltpu.CompilerParams(
            dimension_semantics=("arbitrary",), vmem_limit_bytes=VMEM_LIMIT),
        name="layer_sample",
    )(xs, mod, P["norm1_g"], P["norm2_g"], P["final_g"], W["w_in"], P["pool_bd"], P["pool_scale"],
      P["conv_w"], P["conv_b"], P["conv_ln_g"], P["conv_ln_b"], P["gmlp_ln_g"], P["gmlp_ln_b"],
      P["ws0"], P["bs0"], W["w_out"], W["w_ff1"], W["w_ff3"], W["w_ff2"], state_pool, state_conv)
    bf16_w = dict(zip(_BIG_WEIGHTS, outs[4:])) if cast else None
    return outs[0], outs[1], outs[2], outs[3], bf16_w


def _prep_params(norm1_g, norm2_g, pool_w, pool_scale, conv_w, conv_b, conv_ln_g, conv_ln_b,
                 gmlp_ln_g, gmlp_ln_b, gmlp_ws, gmlp_bs, final_g):
    depth = pool_w.shape[0]
    n_grp = len(POOL_WINDOWS)
    eye = jnp.eye(n_grp, dtype=_F32)
    pool_bd = (eye[None, :, None, :, None] * pool_w[:, :, :, None, :]).reshape(depth, D_POOL, D_POOL)
    mask = jnp.tril(jnp.ones((CHUNK, CHUNK), dtype=bool))
    ws = jnp.where(mask, gmlp_ws, jnp.zeros_like(gmlp_ws))
    ws_pairs = ws.reshape(depth, N_GMLP_HEADS // 2, 2, CHUNK, CHUNK).transpose(0, 1, 3, 2, 4).reshape(
        depth, N_GMLP_HEADS // 2, CHUNK, 2 * CHUNK)
    bs_full = jnp.repeat(gmlp_bs.transpose(0, 2, 1), GMLP_HEAD, axis=2)
    ws0 = jnp.repeat(gmlp_ws[:, :, 0, 0], GMLP_HEAD, axis=1)
    bs0 = jnp.repeat(gmlp_bs[:, :, 0], GMLP_HEAD, axis=1)
    return dict(
        norm1_g=norm1_g, norm2_g=norm2_g, final_g=final_g.reshape(1, D_MODEL),
        pool_bd=pool_bd.astype(_BF16), pool_scale=pool_scale,
        conv_w=conv_w, conv_b=conv_b, conv_ln_g=conv_ln_g, conv_ln_b=conv_ln_b,
        gmlp_ln_g=gmlp_ln_g, gmlp_ln_b=gmlp_ln_b,
        ws_pairs=ws_pairs.astype(_BF16), bs_full=bs_full, ws0=ws0, bs0=bs0,
    )


def kernel(x_prompt, x_sample, c_prompt, c_sample, state_pool, state_conv, w_ada, b_ada, norm1_g, norm2_g, w_in, pool_w, pool_scale, conv_w, conv_b, conv_ln_g, conv_ln_b, gmlp_ln_g, gmlp_ln_b, gmlp_ws, gmlp_bs, w_out, w_ff1, w_ff3, w_ff2, final_g):
    depth = w_in.shape[0]
    nb = x_prompt.shape[0]
    ns = x_sample.shape[0]
    assert x_sample.shape[1] == 1 and x_prompt.shape[1] % TOKEN_TILE == 0 and ns % nb == 0

    c_all = jnp.concatenate([c_sample, c_prompt], axis=0)
    mod = _ada_call(c_all, w_ada, b_ada)
    P = _prep_params(norm1_g, norm2_g, pool_w, pool_scale, conv_w, conv_b, conv_ln_g, conv_ln_b,
                     gmlp_ln_g, gmlp_ln_b, gmlp_ws, gmlp_bs, final_g)
    big_f32 = dict(w_in=w_in, w_out=w_out, w_ff1=w_ff1, w_ff3=w_ff3, w_ff2=w_ff2)

    xp = x_prompt
    xs = x_sample.reshape(ns, D_MODEL)
    spool_t = state_pool.transpose(0, 2, 1, 3)
    sconv_t = state_conv.transpose(0, 2, 1, 3)
    pool_p, conv_p, pool_s, conv_s, v_s = [], [], [], [], []
    W = None
    for l in range(depth):
        last = l == depth - 1
        xs, npool_s, nconv_s, nv_s, cast_w = _layer_sample_call(
            l, xs, mod, P, big_f32 if W is None else W, spool_t, sconv_t, last, cast=W is None)
        W = cast_w if W is None else W
        pool_s.append(npool_s)
        conv_s.append(nconv_s)
        v_s.append(nv_s.reshape(ns, 1, D_GMLP))
        xp, npool_p, nconv_p, W, stacks_s = _layer_prompt_call(
            l, xp, mod, ns, P, W, None if last else big_f32, last, (pool_s, conv_s) if last else None)
        pool_p.append(npool_p)
        conv_p.append(nconv_p)
    return (xp, xs.reshape(ns, 1, D_MODEL), jnp.stack(pool_p), jnp.stack(conv_p),
            stacks_s[0].transpose(0, 2, 1, 3), stacks_s[1].transpose(0, 2, 1, 3), jnp.stack(v_s))
```
